```python
import jax, jax.numpy as jnp
from jax import lax
import numpy as np

D_MODEL = 1024
BATCH = 16
SEQ = 2048
DEPTH = 2

A_GROUPS = 8
A_GROUP_DIM = D_MODEL // 16
A_DIM = A_GROUPS * A_GROUP_DIM
B_GROUPS = 8
B_GROUP_DIM = D_MODEL // 16
B_DIM = B_GROUPS * B_GROUP_DIM
MIX_DIM = A_DIM + B_DIM
IN_EVEN = 2 * A_DIM + 3 * B_DIM
A_CONV_WIDTH = 31
B_CONV_WIDTH = 3
CHUNK = 128
C_GROUPS = 8
C_GROUP_DIM = D_MODEL // 8
C_DIM = C_GROUPS * C_GROUP_DIM
D_FF = 4 * D_MODEL
N_EVEN = (DEPTH + 1) // 2
N_ODD = DEPTH // 2
RMS_EPS = 1e-6
LN_EPS = 1e-5

kernel_name = "hybrid_conformer_shortconv_gmlp_trunk"


def rms_norm(x, g):
    xf = x.astype(jnp.float32)
    y = xf * lax.rsqrt(jnp.mean(xf * xf, axis=-1, keepdims=True) + RMS_EPS)
    return (y * g.astype(jnp.float32)).astype(x.dtype)


def layer_norm(x, g, b):
    xf = x.astype(jnp.float32)
    mu = jnp.mean(xf, axis=-1, keepdims=True)
    xc = xf - mu
    var = jnp.mean(xc * xc, axis=-1, keepdims=True)
    y = xc * lax.rsqrt(var + LN_EPS) * g.astype(jnp.float32) + b.astype(jnp.float32)
    return y.astype(x.dtype)


def causal_depthwise_conv(x, w):
    k = w.shape[0]
    return lax.conv_general_dilated(
        x, w[:, None, :].astype(x.dtype), window_strides=(1,), padding=[(k - 1, 0)],
        dimension_numbers=("NWC", "WIO", "NWC"), feature_group_count=x.shape[-1])


def conv_mixers(h, w_in, conv_a_w, conv_a_b, ln_a_g, ln_a_b, conv_b_w, w_out):
    z = h @ w_in
    a_val, a_gate, b_gate, c_gate, b_val = jnp.split(
        z, [A_DIM, 2 * A_DIM, 2 * A_DIM + B_DIM, 2 * A_DIM + 2 * B_DIM], axis=-1)
    a = a_val * jax.nn.sigmoid(a_gate)
    a = causal_depthwise_conv(a, conv_a_w) + conv_a_b
    a = jax.nn.silu(layer_norm(a, ln_a_g, ln_a_b))
    bo = b_gate * causal_depthwise_conv(c_gate * b_val, conv_b_w)
    return jnp.concatenate([a, bo], axis=-1) @ w_out


def chunked_spatial_gating(h, w_in, b_in, ln_v_g, ln_v_b, w_s, b_s, w_out):
    z = jax.nn.gelu(h @ w_in + b_in, approximate=False)
    u, v = jnp.split(z, 2, axis=-1)
    v = layer_norm(v, ln_v_g, ln_v_b)
    bsz, s, _ = v.shape
    vc = v.reshape(bsz, s // CHUNK, CHUNK, C_GROUPS, C_GROUP_DIM)
    mask = jnp.tril(jnp.ones((CHUNK, CHUNK), dtype=bool))
    ws = jnp.where(mask[None], w_s, 0.0).astype(v.dtype)
    sv = jnp.einsum("gts,bnsgc->bntgc", ws, vc) + b_s.T[None, None, :, :, None].astype(v.dtype)
    y = u * sv.reshape(bsz, s, C_DIM)
    return y @ w_out


def squared_relu_mlp(h, w1, w2):
    a = jax.nn.relu(h @ w1)
    return (a * a) @ w2


def setup_inputs(seed: int = 0) -> dict:
    key = jax.random.key(seed)
    ks = iter(jax.random.split(key, 32))

    def nrm(shape, scale):
        return jax.random.normal(next(ks), shape, jnp.float32) * scale

    def gain(shape):
        return 1.0 + nrm(shape, 0.02)

    d = D_MODEL
    return {
        "x": nrm((BATCH, SEQ, d), 1.0),
        "ev_norm_g": gain((N_EVEN, d)),
        "ev_w_in": nrm((N_EVEN, d, IN_EVEN), d ** -0.5),
        "ev_conv_a_w": nrm((N_EVEN, A_CONV_WIDTH, A_DIM), A_CONV_WIDTH ** -0.5),
        "ev_conv_a_b": nrm((N_EVEN, A_DIM), 0.02),
        "ev_ln_a_g": gain((N_EVEN, A_DIM)),
        "ev_ln_a_b": nrm((N_EVEN, A_DIM), 0.02),
        "ev_conv_b_w": nrm((N_EVEN, B_CONV_WIDTH, B_DIM), B_CONV_WIDTH ** -0.5),
        "ev_w_out": nrm((N_EVEN, MIX_DIM, d), MIX_DIM ** -0.5),
        "od_norm_g": gain((N_ODD, d)),
        "od_w_in": nrm((N_ODD, d, 2 * C_DIM), d ** -0.5),
        "od_b_in": nrm((N_ODD, 2 * C_DIM), 0.02),
        "od_ln_v_g": gain((N_ODD, C_DIM)),
        "od_ln_v_b": nrm((N_ODD, C_DIM), 0.02),
        "od_w_s": nrm((N_ODD, C_GROUPS, CHUNK, CHUNK), CHUNK ** -0.5),
        "od_b_s": gain((N_ODD, C_GROUPS, CHUNK)),
        "od_w_out": nrm((N_ODD, C_DIM, d), C_DIM ** -0.5),
        "mlp_norm_g": gain((DEPTH, d)),
        "mlp_w1": nrm((DEPTH, d, D_FF), d ** -0.5),
        "mlp_w2": nrm((DEPTH, D_FF, d), D_FF ** -0.5),
        "final_norm_g": gain((d,)),
    }


def reference(x, ev_norm_g, ev_w_in, ev_conv_a_w, ev_conv_a_b, ev_ln_a_g, ev_ln_a_b,
              ev_conv_b_w, ev_w_out, od_norm_g, od_w_in, od_b_in, od_ln_v_g, od_ln_v_b,
              od_w_s, od_b_s, od_w_out, mlp_norm_g, mlp_w1, mlp_w2, final_norm_g):
    h = x
    for i in range(DEPTH):
        j = i // 2
        if i % 2 == 0:
            h = h + conv_mixers(rms_norm(h, ev_norm_g[j]), ev_w_in[j], ev_conv_a_w[j],
                                ev_conv_a_b[j], ev_ln_a_g[j], ev_ln_a_b[j],
                                ev_conv_b_w[j], ev_w_out[j])
        else:
            h = h + chunked_spatial_gating(rms_norm(h, od_norm_g[j]), od_w_in[j], od_b_in[j],
                                           od_ln_v_g[j], od_ln_v_b[j], od_w_s[j],
                                           od_b_s[j], od_w_out[j])
        h = h + squared_relu_mlp(rms_norm(h, mlp_norm_g[i]), mlp_w1[i], mlp_w2[i])
    return rms_norm(h, final_norm_g)
```

```python
import functools

import jax
import jax.numpy as jnp
from jax import lax
from jax.experimental import pallas as pl
from jax.experimental.pallas import tpu as pltpu

F32 = jnp.float32
BF16 = jnp.bfloat16

RMS_EPS = 1e-6
LN_EPS = 1e-5

SUBLANES = 8
LANES = 128
VMEM_LIMIT_BYTES = 56 * 1024 * 1024

TOKEN_TILE = 512
FF_CHUNK = 512
CONV_ROWS = 32


def _rms_norm(x, g):
    ms = jnp.mean(x * x, axis=-1, keepdims=True)
    return x * lax.rsqrt(ms + RMS_EPS) * g


def _layer_norm(x, g, b):
    mu = jnp.mean(x, axis=-1, keepdims=True)
    xc = x - mu
    var = jnp.mean(xc * xc, axis=-1, keepdims=True)
    return xc * lax.rsqrt(var + LN_EPS) * g + b


def _dot(a, b):
    return jnp.dot(a, b, preferred_element_type=F32)


def _resident(shape):
    zeros = (0,) * len(shape)
    return pl.BlockSpec(shape, lambda *_: zeros, pipeline_mode=pl.Buffered(1))


def _halo(width):
    return -(-(width - 1) // SUBLANES) * SUBLANES


def _mlp_body(h_ref, g_ref, w1_ref, w2_ref, *rest, final_norm):
    if final_norm:
        gf_ref, o_ref, xn_ref, acc_ref = rest
    else:
        o_ref, xn_ref, acc_ref = rest
    xn_ref[...] = _rms_norm(h_ref[...], g_ref[...]).astype(BF16)
    for c in range(w1_ref.shape[0]):
        a = jnp.maximum(_dot(xn_ref[...], w1_ref[c]), 0.0)
        upd = _dot((a * a).astype(BF16), w2_ref[c])
        if c == 0:
            acc_ref[...] = upd
        else:
            acc_ref[...] += upd
    y = h_ref[...] + acc_ref[...]
    if final_norm:
        y = _rms_norm(y, gf_ref[...])
    o_ref[...] = y


def _mlp(h, g, w1, w2, final_g=None):
    t, d = h.shape
    d_ff = w1.shape[1]
    n_chunks = d_ff // FF_CHUNK
    w1c = w1.astype(BF16).reshape(d, n_chunks, FF_CHUNK).transpose(1, 0, 2)
    w2c = w2.astype(BF16).reshape(n_chunks, FF_CHUNK, d)
    final_norm = final_g is not None
    args = [h, g.reshape(1, d), w1c, w2c]
    in_specs = [
        pl.BlockSpec((TOKEN_TILE, d), lambda i: (i, 0)),
        _resident((1, d)),
        _resident(w1c.shape),
        _resident(w2c.shape),
    ]
    if final_norm:
        args.append(final_g.reshape(1, d))
        in_specs.append(_resident((1, d)))
    return pl.pallas_call(
        functools.partial(_mlp_body, final_norm=final_norm),
        grid=(t // TOKEN_TILE,),
        in_specs=in_specs,
        out_specs=pl.BlockSpec((TOKEN_TILE, d), lambda i: (i, 0)),
        out_shape=jax.ShapeDtypeStruct((t, d), F32),
        scratch_shapes=[
            pltpu.VMEM((TOKEN_TILE, d), BF16),
            pltpu.VMEM((TOKEN_TILE, d), F32),
        ],
        compiler_params=pltpu.CompilerParams(
            dimension_semantics=("arbitrary",), vmem_limit_bytes=VMEM_LIMIT_BYTES),
        name="mlp_final" if final_norm else "mlp",
    )(*args)


def _conv_mix_body(h_ref, g_ref, win_ref, caw_ref, cab_ref, lng_ref, lnb_ref, cbw_ref, wout_ref,
                   o_ref, xn_ref, abuf_ref, cbuf_ref, mix_ref, *, a_dim, b_dim):
    ts = h_ref.shape[1]
    ka = caw_ref.shape[0]
    kb = cbw_ref.shape[0]
    a_halo = _halo(ka)
    b_halo = _halo(kb)

    @pl.when(pl.program_id(1) == 0)
    def _():
        abuf_ref[0:a_halo, :] = jnp.zeros((a_halo, a_dim), F32)
        cbuf_ref[0:b_halo, :] = jnp.zeros((b_halo, b_dim), F32)

    xn_ref[...] = _rms_norm(h_ref[0], g_ref[...]).astype(BF16)

    za = _dot(xn_ref[...], win_ref[:, 0:2 * a_dim])
    abuf_ref[a_halo:a_halo + ts, :] = za[:, 0:a_dim] * jax.nn.sigmoid(za[:, a_dim:2 * a_dim])
    zb = _dot(xn_ref[...], win_ref[:, 2 * a_dim + b_dim:2 * a_dim + 3 * b_dim])
    cbuf_ref[b_halo:b_halo + ts, :] = zb[:, 0:b_dim] * zb[:, b_dim:2 * b_dim]
    b_gate = _dot(xn_ref[...], win_ref[:, 2 * a_dim:2 * a_dim + b_dim])

    rb = CONV_ROWS
    for blk in range(ts // rb):
        r0 = a_halo + blk * rb
        acc = jnp.broadcast_to(cab_ref[...], (rb, a_dim))
        for k in range(ka):
            d = ka - 1 - k
            acc = acc + abuf_ref[r0 - d:r0 - d + rb, :] * caw_ref[k:k + 1, :]
        y = _layer_norm(acc, lng_ref[...], lnb_ref[...])
        mix_ref[blk * rb:(blk + 1) * rb, 0:a_dim] = (y * jax.nn.sigmoid(y)).astype(BF16)
        r0 = b_halo + blk * rb
        acc = None
        for k in range(kb):
            d = kb - 1 - k
            term = cbuf_ref[r0 - d:r0 - d + rb, :] * cbw_ref[k:k + 1, :]
            acc = term if acc is None else acc + term
        mix_ref[blk * rb:(blk + 1) * rb, a_dim:a_dim + b_dim] = (
            b_gate[blk * rb:(blk + 1) * rb, :] * acc).astype(BF16)

    abuf_ref[0:a_halo, :] = abuf_ref[ts:ts + a_halo, :]
    cbuf_ref[0:b_halo, :] = cbuf_ref[ts:ts + b_halo, :]

    o_ref[0] = h_ref[0] + _dot(mix_ref[...], wout_ref[...])


def _conv_mix(h, g, w_in, conv_a_w, conv_a_b, ln_a_g, ln_a_b, conv_b_w, w_out):
    bsz, s, d = h.shape
    ka, a_dim = conv_a_w.shape
    kb, b_dim = conv_b_w.shape
    a_halo = _halo(ka)
    b_halo = _halo(kb)
    ts = TOKEN_TILE
    args = (h, g.reshape(1, d), w_in.astype(BF16), conv_a_w, conv_a_b.reshape(1, a_dim),
            ln_a_g.reshape(1, a_dim), ln_a_b.reshape(1, a_dim), conv_b_w, w_out.astype(BF16))
    in_specs = [pl.BlockSpec((1, ts, d), lambda b, j: (b, j, 0))]
    in_specs += [_resident(a.shape) for a in args[1:]]
    return pl.pallas_call(
        functools.partial(_conv_mix_body, a_dim=a_dim, b_dim=b_dim),
        grid=(bsz, s // ts),
        in_specs=in_specs,
        out_specs=pl.BlockSpec((1, ts, d), lambda b, j: (b, j, 0)),
        out_shape=jax.ShapeDtypeStruct((bsz, s, d), F32),
        scratch_shapes=[
            pltpu.VMEM((ts, d), BF16),
            pltpu.VMEM((a_halo + ts, a_dim), F32),
            pltpu.VMEM((b_halo + ts, b_dim), F32),
            pltpu.VMEM((ts, a_dim + b_dim), BF16),
        ],
        compiler_params=pltpu.CompilerParams(
            dimension_semantics=("arbitrary", "arbitrary"), vmem_limit_bytes=VMEM_LIMIT_BYTES),
        name="conv_mix",
    )(*args)


def _gelu(x):
    return 0.5 * x * (1.0 + lax.erf(x * (0.5 ** 0.5)))


def _spatial_body(h_ref, g_ref, win_ref, bin_ref, lng_ref, lnb_ref, ws_ref, bs_ref, wout_ref,
                  o_ref, xn_ref, u_ref, v_ref, y_ref, *, c_dim):
    tm = h_ref.shape[0]
    groups, chunk, _ = ws_ref.shape
    gdim = c_dim // groups
    n_chunks = tm // chunk

    xn_ref[...] = _rms_norm(h_ref[...], g_ref[...]).astype(BF16)
    u_ref[...] = _gelu(_dot(xn_ref[...], win_ref[:, 0:c_dim]) + bin_ref[:, 0:c_dim])
    v = _gelu(_dot(xn_ref[...], win_ref[:, c_dim:2 * c_dim]) + bin_ref[:, c_dim:2 * c_dim])
    v_ref[...] = _layer_norm(v, lng_ref[...], lnb_ref[...]).astype(BF16)

    row = lax.broadcasted_iota(jnp.int32, (chunk, chunk), 0)
    col = lax.broadcasted_iota(jnp.int32, (chunk, chunk), 1)
    causal = col <= row
    for gi in range(groups):
        lanes = slice(gi * gdim, (gi + 1) * gdim)
        ws = jnp.where(causal, ws_ref[gi], 0.0).astype(BF16)
        vg = jnp.concatenate(
            [v_ref[n * chunk:(n + 1) * chunk, lanes] for n in range(n_chunks)], axis=1)
        sv = _dot(ws, vg)
        for n in range(n_chunks):
            rows = slice(n * chunk, (n + 1) * chunk)
            gate = sv[:, n * gdim:(n + 1) * gdim] + bs_ref[gi]
            y_ref[rows, lanes] = (u_ref[rows, lanes] * gate).astype(BF16)

    o_ref[...] = h_ref[...] + _dot(y_ref[...], wout_ref[...])


def _spatial_gating(h, g, w_in, b_in, ln_v_g, ln_v_b, w_s, b_s, w_out):
    t, d = h.shape
    c_dim = w_out.shape[0]
    groups, chunk, _ = w_s.shape
    gdim = c_dim // groups
    bs = jnp.broadcast_to(b_s[:, :, None], (groups, chunk, gdim))
    args = (h, g.reshape(1, d), w_in.astype(BF16), b_in.reshape(1, 2 * c_dim),
            ln_v_g.reshape(1, c_dim), ln_v_b.reshape(1, c_dim), w_s, bs, w_out.astype(BF16))
    in_specs = [pl.BlockSpec((TOKEN_TILE, d), lambda i: (i, 0))]
    in_specs += [_resident(a.shape) for a in args[1:]]
    return pl.pallas_call(
        functools.partial(_spatial_body, c_dim=c_dim),
        grid=(t // TOKEN_TILE,),
        in_specs=in_specs,
        out_specs=pl.BlockSpec((TOKEN_TILE, d), lambda i: (i, 0)),
        out_shape=jax.ShapeDtypeStruct((t, d), F32),
        scratch_shapes=[
            pltpu.VMEM((TOKEN_TILE, d), BF16),
            pltpu.VMEM((TOKEN_TILE, c_dim), F32),
            pltpu.VMEM((TOKEN_TILE, c_dim), BF16),
            pltpu.VMEM((TOKEN_TILE, c_dim), BF16),
        ],
        compiler_params=pltpu.CompilerParams(
            dimension_semantics=("arbitrary",), vmem_limit_bytes=VMEM_LIMIT_BYTES),
        name="spatial_gating",
    )(*args)


def kernel(x, ev_norm_g, ev_w_in, ev_conv_a_w, ev_conv_a_b, ev_ln_a_g, ev_ln_a_b, ev_conv_b_w, ev_w_out, od_norm_g, od_w_in, od_b_in, od_ln_v_g, od_ln_v_b, od_w_s, od_b_s, od_w_out, mlp_norm_g, mlp_w1, mlp_w2, final_norm_g):
    bsz, s, d = x.shape
    depth = mlp_w1.shape[0]
    h = x
    for i in range(depth):
        j = i // 2
        if i % 2 == 0:
            h = _conv_mix(h.reshape(bsz, s, d), ev_norm_g[j], ev_w_in[j], ev_conv_a_w[j],
                          ev_conv_a_b[j], ev_ln_a_g[j], ev_ln_a_b[j], ev_conv_b_w[j], ev_w_out[j])
        else:
            h = _spatial_gating(h.reshape(bsz * s, d), od_norm_g[j], od_w_in[j], od_b_in[j],
                                od_ln_v_g[j], od_ln_v_b[j], od_w_s[j], od_b_s[j], od_w_out[j])
        h = _mlp(h.reshape(bsz * s, d), mlp_norm_g[i], mlp_w1[i], mlp_w2[i],
                 final_norm_g if i == depth - 1 else None)
    return h.reshape(bsz, s, d)
```

```python
import functools

import jax
import jax.numpy as jnp
from jax import lax
from jax.experimental import pallas as pl
from jax.experimental.pallas import tpu as pltpu

F32 = jnp.float32
BF16 = jnp.bfloat16

RMS_EPS = 1e-6
LN_EPS = 1e-5

SUBLANES = 8
LANES = 128
VMEM_LIMIT_BYTES = 56 * 1024 * 1024

TOKEN_TILE = 512
FF_CHUNK = 512
CONV_ROWS = 32
HIST_ROW_STRIDE = 2


def _rms_norm(x, g):
    ms = jnp.mean(x * x, axis=-1, keepdims=True)
    return x * lax.rsqrt(ms + RMS_EPS) * g


def _layer_norm(x, g, b):
    mu = jnp.mean(x, axis=-1, keepdims=True)
    xc = x - mu
    var = jnp.mean(xc * xc, axis=-1, keepdims=True)
    return xc * lax.rsqrt(var + LN_EPS) * g + b


def _dot(a, b):
    return jnp.dot(a, b, preferred_element_type=F32)


def _resident(shape):
    zeros = (0,) * len(shape)
    return pl.BlockSpec(shape, lambda *_: zeros, pipeline_mode=pl.Buffered(1))


def _halo(width):
    return -(-(width - 1) // SUBLANES) * SUBLANES


def _mlp_body(h_ref, g_ref, w1_ref, w2_ref, *rest, final_norm):
    if final_norm:
        gf_ref, o_ref, xn_ref, acc_ref = rest
    else:
        o_ref, xn_ref, acc_ref = rest
    xn_ref[...] = _rms_norm(h_ref[...], g_ref[...]).astype(BF16)
    fc = FF_CHUNK
    for c in range(w1_ref.shape[1] // fc):
        a = jnp.maximum(_dot(xn_ref[...], w1_ref[:, c * fc:(c + 1) * fc]), 0.0)
        upd = _dot((a * a).astype(BF16), w2_ref[c * fc:(c + 1) * fc, :])
        if c == 0:
            acc_ref[...] = upd
        else:
            acc_ref[...] += upd
    y = h_ref[...] + acc_ref[...]
    if final_norm:
        y = _rms_norm(y, gf_ref[...])
    o_ref[...] = y


def _mlp(h, g, w1, w2, final_g=None):
    t, d = h.shape
    final_norm = final_g is not None
    args = [h, g.reshape(1, d), w1.astype(BF16), w2.astype(BF16)]
    in_specs = [
        pl.BlockSpec((TOKEN_TILE, d), lambda i: (i, 0)),
        _resident((1, d)),
        _resident(w1.shape),
        _resident(w2.shape),
    ]
    if final_norm:
        args.append(final_g.reshape(1, d))
        in_specs.append(_resident((1, d)))
    return pl.pallas_call(
        functools.partial(_mlp_body, final_norm=final_norm),
        grid=(t // TOKEN_TILE,),
        in_specs=in_specs,
        out_specs=pl.BlockSpec((TOKEN_TILE, d), lambda i: (i, 0)),
        out_shape=jax.ShapeDtypeStruct((t, d), F32),
        scratch_shapes=[
            pltpu.VMEM((TOKEN_TILE, d), BF16),
            pltpu.VMEM((TOKEN_TILE, d), F32),
        ],
        compiler_params=pltpu.CompilerParams(
            dimension_semantics=("arbitrary",), vmem_limit_bytes=VMEM_LIMIT_BYTES),
        name="mlp_final" if final_norm else "mlp",
    )(*args)


def _hist_rows(start, size):
    return pl.ds(HIST_ROW_STRIDE * start, size, stride=HIST_ROW_STRIDE)


def _hist_store(buf_ref, start, x):
    for j in range(buf_ref.shape[0]):
        buf_ref[j, _hist_rows(start, x.shape[0]), :] = x[:, j * LANES:(j + 1) * LANES]


def _hist_carry(buf_ref, halo, ts):
    for j in range(buf_ref.shape[0]):
        buf_ref[j, _hist_rows(0, halo), :] = buf_ref[j, _hist_rows(ts, halo), :]


def _causal_conv(buf_ref, w_ref, halo, row0, rows):
    taps = w_ref.shape[0]
    outs = []
    for j in range(buf_ref.shape[0]):
        acc = None
        for k in range(taps):
            start = halo + row0 - (taps - 1 - k)
            term = buf_ref[j, _hist_rows(start, rows), :] * w_ref[k:k + 1, j * LANES:(j + 1) * LANES]
            acc = term if acc is None else acc + term
        outs.append(acc)
    return jnp.concatenate(outs, axis=1)


def _conv_mix_body(h_ref, g_ref, win_ref, caw_ref, cab_ref, lng_ref, lnb_ref, cbw_ref, wout_ref,
                   o_ref, xn_ref, abuf_ref, cbuf_ref, mix_ref, *, a_dim, b_dim):
    ts = h_ref.shape[1]
    a_halo = _halo(caw_ref.shape[0])
    b_halo = _halo(cbw_ref.shape[0])

    @pl.when(pl.program_id(1) == 0)
    def _():
        _hist_store(abuf_ref, 0, jnp.zeros((a_halo, a_dim), F32))
        _hist_store(cbuf_ref, 0, jnp.zeros((b_halo, b_dim), F32))

    xn_ref[...] = _rms_norm(h_ref[0], g_ref[...]).astype(BF16)

    za = _dot(xn_ref[...], win_ref[:, 0:2 * a_dim])
    _hist_store(abuf_ref, a_halo, za[:, 0:a_dim] * jax.nn.sigmoid(za[:, a_dim:2 * a_dim]))
    zb = _dot(xn_ref[...], win_ref[:, 2 * a_dim + b_dim:2 * a_dim + 3 * b_dim])
    _hist_store(cbuf_ref, b_halo, zb[:, 0:b_dim] * zb[:, b_dim:2 * b_dim])
    b_gate = _dot(xn_ref[...], win_ref[:, 2 * a_dim:2 * a_dim + b_dim])

    rb = CONV_ROWS
    for blk in range(ts // rb):
        rows = slice(blk * rb, (blk + 1) * rb)
        ya = _causal_conv(abuf_ref, caw_ref, a_halo, blk * rb, rb) + cab_ref[...]
        ya = _layer_norm(ya, lng_ref[...], lnb_ref[...])
        mix_ref[rows, 0:a_dim] = (ya * jax.nn.sigmoid(ya)).astype(BF16)
        yb = _causal_conv(cbuf_ref, cbw_ref, b_halo, blk * rb, rb)
        mix_ref[rows, a_dim:a_dim + b_dim] = (b_gate[rows, :] * yb).astype(BF16)

    _hist_carry(abuf_ref, a_halo, ts)
    _hist_carry(cbuf_ref, b_halo, ts)

    o_ref[0] = h_ref[0] + _dot(mix_ref[...], wout_ref[...])


def _conv_mix(h, g, w_in, conv_a_w, conv_a_b, ln_a_g, ln_a_b, conv_b_w, w_out):
    bsz, s, d = h.shape
    ka, a_dim = conv_a_w.shape
    kb, b_dim = conv_b_w.shape
    a_halo = _halo(ka)
    b_halo = _halo(kb)
    ts = TOKEN_TILE
    args = (h, g.reshape(1, d), w_in.astype(BF16), conv_a_w, conv_a_b.reshape(1, a_dim),
            ln_a_g.reshape(1, a_dim), ln_a_b.reshape(1, a_dim), conv_b_w, w_out.astype(BF16))
    in_specs = [pl.BlockSpec((1, ts, d), lambda b, j: (b, j, 0))]
    in_specs += [_resident(a.shape) for a in args[1:]]
    return pl.pallas_call(
        functools.partial(_conv_mix_body, a_dim=a_dim, b_dim=b_dim),
        grid=(bsz, s // ts),
        in_specs=in_specs,
        out_specs=pl.BlockSpec((1, ts, d), lambda b, j: (b, j, 0)),
        out_shape=jax.ShapeDtypeStruct((bsz, s, d), F32),
        scratch_shapes=[
            pltpu.VMEM((ts, d), BF16),
            pltpu.VMEM((a_dim // LANES, HIST_ROW_STRIDE * (a_halo + ts), LANES), F32),
            pltpu.VMEM((b_dim // LANES, HIST_ROW_STRIDE * (b_halo + ts), LANES), F32),
            pltpu.VMEM((ts, a_dim + b_dim), BF16),
        ],
        compiler_params=pltpu.CompilerParams(
            dimension_semantics=("arbitrary", "arbitrary"), vmem_limit_bytes=VMEM_LIMIT_BYTES),
        name="conv_mix",
    )(*args)


def _gelu(x):
    return 0.5 * x * (1.0 + lax.erf(x * (0.5 ** 0.5)))


def _spatial_body(h_ref, g_ref, win_ref, bin_ref, lng_ref, lnb_ref, ws_ref, bs_ref, wout_ref,
                  o_ref, xn_ref, u_ref, v_ref, y_ref, *, c_dim):
    tm = h_ref.shape[0]
    groups, chunk, _ = ws_ref.shape
    gdim = c_dim // groups
    n_chunks = tm // chunk

    xn_ref[...] = _rms_norm(h_ref[...], g_ref[...]).astype(BF16)
    u_ref[...] = _gelu(_dot(xn_ref[...], win_ref[:, 0:c_dim]) + bin_ref[:, 0:c_dim])
    v = _gelu(_dot(xn_ref[...], win_ref[:, c_dim:2 * c_dim]) + bin_ref[:, c_dim:2 * c_dim])
    v_ref[...] = _layer_norm(v, lng_ref[...], lnb_ref[...]).astype(BF16)

    row = lax.broadcasted_iota(jnp.int32, (chunk, chunk), 0)
    col = lax.broadcasted_iota(jnp.int32, (chunk, chunk), 1)
    causal = col <= row
    for gi in range(groups):
        lanes = slice(gi * gdim, (gi + 1) * gdim)
        ws = jnp.where(causal, ws_ref[gi], 0.0).astype(BF16)
        vg = jnp.concatenate(
            [v_ref[n * chunk:(n + 1) * chunk, lanes] for n in range(n_chunks)], axis=1)
        sv = _dot(ws, vg)
        for n in range(n_chunks):
            rows = slice(n * chunk, (n + 1) * chunk)
            gate = sv[:, n * gdim:(n + 1) * gdim] + bs_ref[gi]
            y_ref[rows, lanes] = (u_ref[rows, lanes] * gate).astype(BF16)

    o_ref[...] = h_ref[...] + _dot(y_ref[...], wout_ref[...])


def _spatial_gating(h, g, w_in, b_in, ln_v_g, ln_v_b, w_s, b_s, w_out):
    t, d = h.shape
    c_dim = w_out.shape[0]
    groups, chunk, _ = w_s.shape
    gdim = c_dim // groups
    bs = jnp.broadcast_to(b_s[:, :, None], (groups, chunk, gdim))
    args = (h, g.reshape(1, d), w_in.astype(BF16), b_in.reshape(1, 2 * c_dim),
            ln_v_g.reshape(1, c_dim), ln_v_b.reshape(1, c_dim), w_s, bs, w_out.astype(BF16))
    in_specs = [pl.BlockSpec((TOKEN_TILE, d), lambda i: (i, 0))]
    in_specs += [_resident(a.shape) for a in args[1:]]
    return pl.pallas_call(
        functools.partial(_spatial_body, c_dim=c_dim),
        grid=(t // TOKEN_TILE,),
        in_specs=in_specs,
        out_specs=pl.BlockSpec((TOKEN_TILE, d), lambda i: (i, 0)),
        out_shape=jax.ShapeDtypeStruct((t, d), F32),
        scratch_shapes=[
            pltpu.VMEM((TOKEN_TILE, d), BF16),
            pltpu.VMEM((TOKEN_TILE, c_dim), F32),
            pltpu.VMEM((TOKEN_TILE, c_dim), BF16),
            pltpu.VMEM((TOKEN_TILE, c_dim), BF16),
        ],
        compiler_params=pltpu.CompilerParams(
            dimension_semantics=("arbitrary",), vmem_limit_bytes=VMEM_LIMIT_BYTES),
        name="spatial_gating",
    )(*args)


def kernel(x, ev_norm_g, ev_w_in, ev_conv_a_w, ev_conv_a_b, ev_ln_a_g, ev_ln_a_b, ev_conv_b_w, ev_w_out, od_norm_g, od_w_in, od_b_in, od_ln_v_g, od_ln_v_b, od_w_s, od_b_s, od_w_out, mlp_norm_g, mlp_w1, mlp_w2, final_norm_g):
    bsz, s, d = x.shape
    depth = mlp_w1.shape[0]
    h = x
    for i in range(depth):
        j = i // 2
        if i % 2 == 0:
            h = _conv_mix(h.reshape(bsz, s, d), ev_norm_g[j], ev_w_in[j], ev_conv_a_w[j],
                          ev_conv_a_b[j], ev_ln_a_g[j], ev_ln_a_b[j], ev_conv_b_w[j], ev_w_out[j])
        else:
            h = _spatial_gating(h.reshape(bsz * s, d), od_norm_g[j], od_w_in[j], od_b_in[j],
                                od_ln_v_g[j], od_ln_v_b[j], od_w_s[j], od_b_s[j], od_w_out[j])
        h = _mlp(h.reshape(bsz * s, d), mlp_norm_g[i], mlp_w1[i], mlp_w2[i],
                 final_norm_g if i == depth - 1 else None)
    return h.reshape(bsz, s, d)
```

```python
import functools

import jax
import jax.numpy as jnp
from jax import lax
from jax.experimental import pallas as pl
from jax.experimental.pallas import tpu as pltpu

F32 = jnp.float32
BF16 = jnp.bfloat16

RMS_EPS = 1e-6
LN_EPS = 1e-5

SUBLANES = 8
LANES = 128
VMEM_LIMIT_BYTES = 56 * 1024 * 1024

TOKEN_TILE = 512
MLP_TOKEN_TILE = 1024
FF_CHUNK = 512
ROW_BLOCK = 256
CONV_ROWS = 32
GLU_COLS = 256
HIST_ROW_STRIDE = 2


def _rms_norm(x, g):
    ms = jnp.mean(x * x, axis=-1, keepdims=True)
    return x * lax.rsqrt(ms + RMS_EPS) * g


def _layer_norm(x, g, b):
    mu = jnp.mean(x, axis=-1, keepdims=True)
    xc = x - mu
    var = jnp.mean(xc * xc, axis=-1, keepdims=True)
    return xc * lax.rsqrt(var + LN_EPS) * g + b


def _dot(a, b):
    return jnp.dot(a, b, preferred_element_type=F32)


def _resident(shape):
    zeros = (0,) * len(shape)
    return pl.BlockSpec(shape, lambda *_: zeros, pipeline_mode=pl.Buffered(1))


def _halo(width):
    return -(-(width - 1) // SUBLANES) * SUBLANES


def _mlp_body(h_ref, g_ref, w1_ref, w2_ref, *rest, final_norm):
    if final_norm:
        gf_ref, o_ref, xn_ref, acc_ref = rest
    else:
        o_ref, xn_ref, acc_ref = rest
    xn_ref[...] = _rms_norm(h_ref[...], g_ref[...]).astype(BF16)
    fc = FF_CHUNK
    for c in range(w1_ref.shape[1] // fc):
        a = jnp.maximum(_dot(xn_ref[...], w1_ref[:, c * fc:(c + 1) * fc]), 0.0)
        upd = _dot((a * a).astype(BF16), w2_ref[c * fc:(c + 1) * fc, :])
        if c == 0:
            acc_ref[...] = upd
        else:
            acc_ref[...] += upd
    y = h_ref[...] + acc_ref[...]
    if final_norm:
        y = _rms_norm(y, gf_ref[...])
    o_ref[...] = y


def _mlp(h, g, w1, w2, final_g=None):
    t, d = h.shape
    final_norm = final_g is not None
    args = [h, g.reshape(1, d), w1.astype(BF16), w2.astype(BF16)]
    in_specs = [
        pl.BlockSpec((MLP_TOKEN_TILE, d), lambda i: (i, 0)),
        _resident((1, d)),
        _resident(w1.shape),
        _resident(w2.shape),
    ]
    if final_norm:
        args.append(final_g.reshape(1, d))
        in_specs.append(_resident((1, d)))
    return pl.pallas_call(
        functools.partial(_mlp_body, final_norm=final_norm),
        grid=(t // MLP_TOKEN_TILE,),
        in_specs=in_specs,
        out_specs=pl.BlockSpec((MLP_TOKEN_TILE, d), lambda i: (i, 0)),
        out_shape=jax.ShapeDtypeStruct((t, d), F32),
        scratch_shapes=[
            pltpu.VMEM((MLP_TOKEN_TILE, d), BF16),
            pltpu.VMEM((MLP_TOKEN_TILE, d), F32),
        ],
        compiler_params=pltpu.CompilerParams(
            dimension_semantics=("arbitrary",), vmem_limit_bytes=VMEM_LIMIT_BYTES),
        name="mlp_final" if final_norm else "mlp",
    )(*args)


def _hist_rows(start, size):
    return pl.ds(HIST_ROW_STRIDE * start, size, stride=HIST_ROW_STRIDE)


def _hist_store(buf_ref, start, x, first_block=0):
    for j in range(x.shape[1] // LANES):
        buf_ref[first_block + j, _hist_rows(start, x.shape[0]), :] = x[:, j * LANES:(j + 1) * LANES]


def _hist_carry(buf_ref, halo, ts, sequence_end):
    for j in range(buf_ref.shape[0]):
        tail = buf_ref[j, _hist_rows(ts, halo), :]
        buf_ref[j, _hist_rows(0, halo), :] = jnp.where(sequence_end, 0.0, tail)


def _causal_conv(buf_ref, w_ref, halo, row0, rows, j):
    taps = w_ref.shape[0]
    acc = None
    for k in range(taps):
        start = halo + row0 - (taps - 1 - k)
        term = buf_ref[j, _hist_rows(start, rows), :] * w_ref[k:k + 1, j * LANES:(j + 1) * LANES]
        acc = term if acc is None else acc + term
    return acc


def _conv_mix_body(h_ref, g_ref, win_ref, caw_ref, cab_ref, lng_ref, lnb_ref, cbw_ref, wout_ref,
                   o_ref, xn_ref, abuf_ref, cbuf_ref, ya_ref, mix_ref, *, a_dim, b_dim, tiles_per_seq):
    ts = h_ref.shape[0]
    a_halo = _halo(caw_ref.shape[0])
    b_halo = _halo(cbw_ref.shape[0])
    s = pl.program_id(0)

    @pl.when(s == 0)
    def _():
        _hist_store(abuf_ref, 0, jnp.zeros((a_halo, a_dim), F32))
        _hist_store(cbuf_ref, 0, jnp.zeros((b_halo, b_dim), F32))

    xn_ref[...] = _rms_norm(h_ref[...], g_ref[...]).astype(BF16)

    n_conv_blocks = ts // CONV_ROWS
    z = _dot(xn_ref[...], win_ref[...])
    for p in range(a_dim // GLU_COLS):
        zp = z[:, 2 * p * GLU_COLS:2 * (p + 1) * GLU_COLS]
        glu = zp[:, 0:GLU_COLS] * jax.nn.sigmoid(zp[:, GLU_COLS:2 * GLU_COLS])
        _hist_store(abuf_ref, a_halo, glu, first_block=p * GLU_COLS // LANES)
    zb = z[:, 2 * a_dim:2 * a_dim + 2 * b_dim]
    _hist_store(cbuf_ref, b_halo, zb[:, 0:b_dim] * zb[:, b_dim:2 * b_dim])
    b_gate = z[:, 2 * a_dim + 2 * b_dim:2 * a_dim + 3 * b_dim]

    for j in range(a_dim // LANES):
        lanes = slice(j * LANES, (j + 1) * LANES)
        for blk in range(n_conv_blocks):
            rows = slice(blk * CONV_ROWS, (blk + 1) * CONV_ROWS)
            ya_ref[rows, lanes] = (
                _causal_conv(abuf_ref, caw_ref, a_halo, blk * CONV_ROWS, CONV_ROWS, j)
                + cab_ref[:, lanes])

    for blk in range(n_conv_blocks):
        rows = slice(blk * CONV_ROWS, (blk + 1) * CONV_ROWS)
        ya = _layer_norm(ya_ref[rows, :], lng_ref[...], lnb_ref[...])
        mix_ref[rows, 0:a_dim] = (ya * jax.nn.sigmoid(ya)).astype(BF16)
        yb = jnp.concatenate(
            [_causal_conv(cbuf_ref, cbw_ref, b_halo, blk * CONV_ROWS, CONV_ROWS, j)
             for j in range(b_dim // LANES)], axis=1)
        mix_ref[rows, a_dim:a_dim + b_dim] = (b_gate[rows, :] * yb).astype(BF16)

    for blk in range(ts // ROW_BLOCK):
        rows = slice(blk * ROW_BLOCK, (blk + 1) * ROW_BLOCK)
        o_ref[rows, :] = h_ref[rows, :] + _dot(mix_ref[rows, :], wout_ref[...])

    sequence_end = (s + 1) % tiles_per_seq == 0
    _hist_carry(abuf_ref, a_halo, ts, sequence_end)
    _hist_carry(cbuf_ref, b_halo, ts, sequence_end)


def _conv_mix(h, g, w_in, conv_a_w, conv_a_b, ln_a_g, ln_a_b, conv_b_w, w_out, *, seq_len):
    t, d = h.shape
    ka, a_dim = conv_a_w.shape
    kb, b_dim = conv_b_w.shape
    a_halo = _halo(ka)
    b_halo = _halo(kb)
    ts = TOKEN_TILE
    assert seq_len % ts == 0 and t % seq_len == 0
    cols = []
    for p in range(a_dim // GLU_COLS):
        cols.append(w_in[:, p * GLU_COLS:(p + 1) * GLU_COLS])
        cols.append(w_in[:, a_dim + p * GLU_COLS:a_dim + (p + 1) * GLU_COLS])
    cols.append(w_in[:, 2 * a_dim + b_dim:2 * a_dim + 3 * b_dim])
    cols.append(w_in[:, 2 * a_dim:2 * a_dim + b_dim])
    w_in_cols = jnp.concatenate(cols, axis=1).astype(BF16)
    weights = (g.reshape(1, d), w_in_cols, conv_a_w, conv_a_b.reshape(1, a_dim),
               ln_a_g.reshape(1, a_dim), ln_a_b.reshape(1, a_dim), conv_b_w, w_out.astype(BF16))
    in_specs = [pl.BlockSpec((ts, d), lambda s: (s, 0))]
    in_specs += [_resident(a.shape) for a in weights]
    return pl.pallas_call(
        functools.partial(_conv_mix_body, a_dim=a_dim, b_dim=b_dim, tiles_per_seq=seq_len // ts),
        grid=(t // ts,),
        in_specs=in_specs,
        out_specs=pl.BlockSpec((ts, d), lambda s: (s, 0)),
        out_shape=jax.ShapeDtypeStruct((t, d), F32),
        scratch_shapes=[
            pltpu.VMEM((ts, d), BF16),
            pltpu.VMEM((a_dim // LANES, HIST_ROW_STRIDE * (a_halo + ts), LANES), F32),
            pltpu.VMEM((b_dim // LANES, HIST_ROW_STRIDE * (b_halo + ts), LANES), F32),
            pltpu.VMEM((ts, a_dim), F32),
            pltpu.VMEM((ts, a_dim + b_dim), BF16),
        ],
        compiler_params=pltpu.CompilerParams(
            dimension_semantics=("arbitrary",), vmem_limit_bytes=VMEM_LIMIT_BYTES),
        name="conv_mix",
    )(h, *weights)


def _gelu(x):
    return 0.5 * x * (1.0 + lax.erf(x * (0.5 ** 0.5)))


def _spatial_body(h_ref, g_ref, win_ref, bin_ref, lng_ref, lnb_ref, ws_ref, bs_ref, wout_ref,
                  o_ref, xn_ref, u_ref, v_ref, y_ref, *, c_dim):
    tm = h_ref.shape[0]
    groups, chunk, _ = ws_ref.shape
    gdim = c_dim // groups
    n_chunks = tm // chunk

    for n in range(tm // ROW_BLOCK):
        rows = slice(n * ROW_BLOCK, (n + 1) * ROW_BLOCK)
        xn_ref[rows, :] = _rms_norm(h_ref[rows, :], g_ref[...]).astype(BF16)
        v = _gelu(_dot(xn_ref[rows, :], win_ref[:, c_dim:2 * c_dim]) + bin_ref[:, c_dim:2 * c_dim])
        v_ref[rows, :] = _layer_norm(v, lng_ref[...], lnb_ref[...]).astype(BF16)
    for n in range(tm // ROW_BLOCK):
        rows = slice(n * ROW_BLOCK, (n + 1) * ROW_BLOCK)
        u_ref[rows, :] = _gelu(_dot(xn_ref[rows, :], win_ref[:, 0:c_dim]) + bin_ref[:, 0:c_dim])

    row = lax.broadcasted_iota(jnp.int32, (chunk, chunk), 0)
    col = lax.broadcasted_iota(jnp.int32, (chunk, chunk), 1)
    causal = col <= row
    sv = []
    for gi in range(groups):
        lanes = slice(gi * gdim, (gi + 1) * gdim)
        ws = jnp.where(causal, ws_ref[gi], 0.0).astype(BF16)
        vg = jnp.concatenate(
            [v_ref[n * chunk:(n + 1) * chunk, lanes] for n in range(n_chunks)], axis=1)
        sv.append(_dot(ws, vg))
    chunks_per_block = ROW_BLOCK // chunk
    for blk in range(tm // ROW_BLOCK):
        y = []
        for n in range(blk * chunks_per_block, (blk + 1) * chunks_per_block):
            gate = jnp.concatenate(
                [sv[gi][:, n * gdim:(n + 1) * gdim] + bs_ref[gi] for gi in range(groups)], axis=1)
            y.append((u_ref[n * chunk:(n + 1) * chunk, :] * gate).astype(BF16))
        rows = slice(blk * ROW_BLOCK, (blk + 1) * ROW_BLOCK)
        o_ref[rows, :] = h_ref[rows, :] + _dot(jnp.concatenate(y, axis=0), wout_ref[...])


def _spatial_gating(h, g, w_in, b_in, ln_v_g, ln_v_b, w_s, b_s, w_out, *, seq_len):
    t, d = h.shape
    c_dim = w_out.shape[0]
    groups, chunk, _ = w_s.shape
    gdim = c_dim // groups
    assert TOKEN_TILE % chunk == 0 and seq_len % TOKEN_TILE == 0
    bs = jnp.broadcast_to(b_s[:, :, None], (groups, chunk, gdim))
    args = (h, g.reshape(1, d), w_in.astype(BF16), b_in.reshape(1, 2 * c_dim),
            ln_v_g.reshape(1, c_dim), ln_v_b.reshape(1, c_dim), w_s, bs, w_out.astype(BF16))
    in_specs = [pl.BlockSpec((TOKEN_TILE, d), lambda i: (i, 0))]
    in_specs += [_resident(a.shape) for a in args[1:]]
    return pl.pallas_call(
        functools.partial(_spatial_body, c_dim=c_dim),
        grid=(t // TOKEN_TILE,),
        in_specs=in_specs,
        out_specs=pl.BlockSpec((TOKEN_TILE, d), lambda i: (i, 0)),
        out_shape=jax.ShapeDtypeStruct((t, d), F32),
        scratch_shapes=[
            pltpu.VMEM((TOKEN_TILE, d), BF16),
            pltpu.VMEM((TOKEN_TILE, c_dim), F32),
            pltpu.VMEM((TOKEN_TILE, c_dim), BF16),
            pltpu.VMEM((TOKEN_TILE, c_dim), BF16),
        ],
        compiler_params=pltpu.CompilerParams(
            dimension_semantics=("arbitrary",), vmem_limit_bytes=VMEM_LIMIT_BYTES),
        name="spatial_gating",
    )(*args)


def kernel(x, ev_norm_g, ev_w_in, ev_conv_a_w, ev_conv_a_b, ev_ln_a_g, ev_ln_a_b, ev_conv_b_w, ev_w_out, od_norm_g, od_w_in, od_b_in, od_ln_v_g, od_ln_v_b, od_w_s, od_b_s, od_w_out, mlp_norm_g, mlp_w1, mlp_w2, final_norm_g):
    bsz, s, d = x.shape
    depth = mlp_w1.shape[0]
    h = x.reshape(bsz * s, d)
    for i in range(depth):
        j = i // 2
        if i % 2 == 0:
            h = _conv_mix(h, ev_norm_g[j], ev_w_in[j], ev_conv_a_w[j], ev_conv_a_b[j], ev_ln_a_g[j],
                          ev_ln_a_b[j], ev_conv_b_w[j], ev_w_out[j], seq_len=s)
        else:
            h = _spatial_gating(h, od_norm_g[j], od_w_in[j], od_b_in[j], od_ln_v_g[j], od_ln_v_b[j],
                                od_w_s[j], od_b_s[j], od_w_out[j], seq_len=s)
        h = _mlp(h, mlp_norm_g[i], mlp_w1[i], mlp_w2[i], final_norm_g if i == depth - 1 else None)
    return h.reshape(bsz, s, d)
```

```python
import functools

import jax
import jax.numpy as jnp
from jax import lax
from jax.experimental import pallas as pl
from jax.experimental.pallas import tpu as pltpu

F32 = jnp.float32
BF16 = jnp.bfloat16

RMS_EPS = 1e-6
LN_EPS = 1e-5

SUBLANES = 8
LANES = 128
BF16_SUBLANES = 16
VMEM_LIMIT_BYTES = 56 * 1024 * 1024

TOKEN_TILE = 512
MLP_TOKEN_TILE = 1024
FF_CHUNK = 512
ROW_BLOCK = 256
CONV_ROWS = 32
GLU_COLS = 256
HIST_ROW_STRIDE = 2


def _rms_norm(x, g):
    ms = jnp.mean(x * x, axis=-1, keepdims=True)
    return x * lax.rsqrt(ms + RMS_EPS) * g


def _layer_norm(x, g, b):
    mu = jnp.mean(x, axis=-1, keepdims=True)
    xc = x - mu
    var = jnp.mean(xc * xc, axis=-1, keepdims=True)
    return xc * lax.rsqrt(var + LN_EPS) * g + b


def _dot(a, b):
    return jnp.dot(a, b, preferred_element_type=F32)


def _resident(shape):
    zeros = (0,) * len(shape)
    return pl.BlockSpec(shape, lambda *_: zeros, pipeline_mode=pl.Buffered(1))


def _halo(width):
    return -(-(width - 1) // SUBLANES) * SUBLANES


def _cast_stream(weights, n_steps):
    in_specs, out_specs, out_shapes = [], [], []
    for w in weights:
        rows, cols = w.shape
        slab = rows // n_steps
        assert slab * n_steps == rows and slab % BF16_SUBLANES == 0
        in_specs.append(pl.BlockSpec((slab, cols), lambda i: (i, 0)))
        out_specs.append(pl.BlockSpec((slab, cols), lambda i: (i, 0)))
        out_shapes.append(jax.ShapeDtypeStruct((rows, cols), BF16))
    return in_specs, out_specs, out_shapes


def _cast_slabs(src_refs, dst_refs):
    for src_ref, dst_ref in zip(src_refs, dst_refs):
        dst_ref[...] = src_ref[...].astype(BF16)


def _mlp_body(h_ref, g_ref, w1_ref, w2_ref, *rest, final_norm, n_cast):
    if final_norm:
        gf_ref, rest = rest[0], rest[1:]
    cast_src, rest = rest[:n_cast], rest[n_cast:]
    o_ref, rest = rest[0], rest[1:]
    cast_dst, (xn_ref, acc_ref) = rest[:n_cast], rest[n_cast:]
    _cast_slabs(cast_src, cast_dst)
    xn_ref[...] = _rms_norm(h_ref[...], g_ref[...]).astype(BF16)
    fc = FF_CHUNK
    for c in range(w1_ref.shape[1] // fc):
        a = jnp.maximum(_dot(xn_ref[...], w1_ref[:, c * fc:(c + 1) * fc]), 0.0)
        upd = _dot((a * a).astype(BF16), w2_ref[c * fc:(c + 1) * fc, :])
        if c == 0:
            acc_ref[...] = upd
        else:
            acc_ref[...] += upd
    y = h_ref[...] + acc_ref[...]
    if final_norm:
        y = _rms_norm(y, gf_ref[...])
    o_ref[...] = y


def _mlp(h, g, w1, w2, final_g=None, cast=()):
    t, d = h.shape
    final_norm = final_g is not None
    n_steps = t // MLP_TOKEN_TILE
    args = [h, g.reshape(1, d), w1, w2]
    in_specs = [
        pl.BlockSpec((MLP_TOKEN_TILE, d), lambda i: (i, 0)),
        _resident((1, d)),
        _resident(w1.shape),
        _resident(w2.shape),
    ]
    if final_norm:
        args.append(final_g.reshape(1, d))
        in_specs.append(_resident((1, d)))
    cast_in, cast_out, cast_shapes = _cast_stream(cast, n_steps)
    outs = pl.pallas_call(
        functools.partial(_mlp_body, final_norm=final_norm, n_cast=len(cast)),
        grid=(n_steps,),
        in_specs=in_specs + cast_in,
        out_specs=[pl.BlockSpec((MLP_TOKEN_TILE, d), lambda i: (i, 0))] + cast_out,
        out_shape=[jax.ShapeDtypeStruct((t, d), F32)] + cast_shapes,
        scratch_shapes=[
            pltpu.VMEM((MLP_TOKEN_TILE, d), BF16),
            pltpu.VMEM((MLP_TOKEN_TILE, d), F32),
        ],
        compiler_params=pltpu.CompilerParams(
            dimension_semantics=("arbitrary",), vmem_limit_bytes=VMEM_LIMIT_BYTES),
        name="mlp_final" if final_norm else "mlp",
    )(*args, *cast)
    return outs[0], tuple(outs[1:])


def _hist_rows(start, size):
    return pl.ds(HIST_ROW_STRIDE * start, size, stride=HIST_ROW_STRIDE)


def _hist_store(buf_ref, start, x, first_block=0):
    for j in range(x.shape[1] // LANES):
        buf_ref[first_block + j, _hist_rows(start, x.shape[0]), :] = x[:, j * LANES:(j + 1) * LANES]


def _hist_carry(buf_ref, halo, ts, sequence_end):
    for j in range(buf_ref.shape[0]):
        tail = buf_ref[j, _hist_rows(ts, halo), :]
        buf_ref[j, _hist_rows(0, halo), :] = jnp.where(sequence_end, 0.0, tail)


def _causal_conv(buf_ref, w_ref, halo, row0, rows, j):
    taps = w_ref.shape[0]
    acc = None
    for k in range(taps):
        start = halo + row0 - (taps - 1 - k)
        term = buf_ref[j, _hist_rows(start, rows), :] * w_ref[k:k + 1, j * LANES:(j + 1) * LANES]
        acc = term if acc is None else acc + term
    return acc


def _conv_mix_body(h_ref, g_ref, win_ref, caw_ref, cab_ref, lng_ref, lnb_ref, cbw_ref, wout_ref,
                   *rest, a_dim, b_dim, tiles_per_seq, n_cast):
    cast_src, rest = rest[:n_cast], rest[n_cast:]
    o_ref, rest = rest[0], rest[1:]
    cast_dst, (xn_ref, abuf_ref, cbuf_ref, ya_ref, mix_ref) = rest[:n_cast], rest[n_cast:]
    _cast_slabs(cast_src, cast_dst)
    ts = h_ref.shape[0]
    a_halo = _halo(caw_ref.shape[0])
    b_halo = _halo(cbw_ref.shape[0])
    s = pl.program_id(0)

    @pl.when(s == 0)
    def _():
        _hist_store(abuf_ref, 0, jnp.zeros((a_halo, a_dim), F32))
        _hist_store(cbuf_ref, 0, jnp.zeros((b_halo, b_dim), F32))

    xn_ref[...] = _rms_norm(h_ref[...], g_ref[...]).astype(BF16)

    n_conv_blocks = ts // CONV_ROWS
    z = _dot(xn_ref[...], win_ref[...])
    for p in range(a_dim // GLU_COLS):
        zp = z[:, 2 * p * GLU_COLS:2 * (p + 1) * GLU_COLS]
        glu = zp[:, 0:GLU_COLS] * jax.nn.sigmoid(zp[:, GLU_COLS:2 * GLU_COLS])
        _hist_store(abuf_ref, a_halo, glu, first_block=p * GLU_COLS // LANES)
    zb = z[:, 2 * a_dim:2 * a_dim + 2 * b_dim]
    _hist_store(cbuf_ref, b_halo, zb[:, 0:b_dim] * zb[:, b_dim:2 * b_dim])
    b_gate = z[:, 2 * a_dim + 2 * b_dim:2 * a_dim + 3 * b_dim]

    for j in range(a_dim // LANES):
        lanes = slice(j * LANES, (j + 1) * LANES)
        for blk in range(n_conv_blocks):
            rows = slice(blk * CONV_ROWS, (blk + 1) * CONV_ROWS)
            ya_ref[rows, lanes] = (
                _causal_conv(abuf_ref, caw_ref, a_halo, blk * CONV_ROWS, CONV_ROWS, j)
                + cab_ref[:, lanes])

    for blk in range(n_conv_blocks):
        rows = slice(blk * CONV_ROWS, (blk + 1) * CONV_ROWS)
        ya = _layer_norm(ya_ref[rows, :], lng_ref[...], lnb_ref[...])
        mix_ref[rows, 0:a_dim] = (ya * jax.nn.sigmoid(ya)).astype(BF16)
        yb = jnp.concatenate(
            [_causal_conv(cbuf_ref, cbw_ref, b_halo, blk * CONV_ROWS, CONV_ROWS, j)
             for j in range(b_dim // LANES)], axis=1)
        mix_ref[rows, a_dim:a_dim + b_dim] = (b_gate[rows, :] * yb).astype(BF16)

    for blk in range(ts // ROW_BLOCK):
        rows = slice(blk * ROW_BLOCK, (blk + 1) * ROW_BLOCK)
        o_ref[rows, :] = h_ref[rows, :] + _dot(mix_ref[rows, :], wout_ref[...])

    sequence_end = (s + 1) % tiles_per_seq == 0
    _hist_carry(abuf_ref, a_halo, ts, sequence_end)
    _hist_carry(cbuf_ref, b_halo, ts, sequence_end)


def _conv_in_columns(w_in, a_dim, b_dim):
    cols = []
    for p in range(a_dim // GLU_COLS):
        cols.append(w_in[:, p * GLU_COLS:(p + 1) * GLU_COLS])
        cols.append(w_in[:, a_dim + p * GLU_COLS:a_dim + (p + 1) * GLU_COLS])
    cols.append(w_in[:, 2 * a_dim + b_dim:2 * a_dim + 3 * b_dim])
    cols.append(w_in[:, 2 * a_dim:2 * a_dim + b_dim])
    return jnp.concatenate(cols, axis=1)


def _conv_mix(h, g, w_in_cols, conv_a_w, conv_a_b, ln_a_g, ln_a_b, conv_b_w, w_out, *, seq_len,
              cast=()):
    t, d = h.shape
    ka, a_dim = conv_a_w.shape
    kb, b_dim = conv_b_w.shape
    a_halo = _halo(ka)
    b_halo = _halo(kb)
    ts = TOKEN_TILE
    assert seq_len % ts == 0 and t % seq_len == 0
    n_steps = t // ts
    weights = (g.reshape(1, d), w_in_cols, conv_a_w, conv_a_b.reshape(1, a_dim),
               ln_a_g.reshape(1, a_dim), ln_a_b.reshape(1, a_dim), conv_b_w, w_out)
    in_specs = [pl.BlockSpec((ts, d), lambda s: (s, 0))]
    in_specs += [_resident(a.shape) for a in weights]
    cast_in, cast_out, cast_shapes = _cast_stream(cast, n_steps)
    outs = pl.pallas_call(
        functools.partial(_conv_mix_body, a_dim=a_dim, b_dim=b_dim, tiles_per_seq=seq_len // ts,
                          n_cast=len(cast)),
        grid=(n_steps,),
        in_specs=in_specs + cast_in,
        out_specs=[pl.BlockSpec((ts, d), lambda s: (s, 0))] + cast_out,
        out_shape=[jax.ShapeDtypeStruct((t, d), F32)] + cast_shapes,
        scratch_shapes=[
            pltpu.VMEM((ts, d), BF16),
            pltpu.VMEM((a_dim // LANES, HIST_ROW_STRIDE * (a_halo + ts), LANES), F32),
            pltpu.VMEM((b_dim // LANES, HIST_ROW_STRIDE * (b_halo + ts), LANES), F32),
            pltpu.VMEM((ts, a_dim), F32),
            pltpu.VMEM((ts, a_dim + b_dim), BF16),
        ],
        compiler_params=pltpu.CompilerParams(
            dimension_semantics=("arbitrary",), vmem_limit_bytes=VMEM_LIMIT_BYTES),
        name="conv_mix",
    )(h, *weights, *cast)
    return outs[0], tuple(outs[1:])


def _gelu(x):
    return 0.5 * x * (1.0 + lax.erf(x * (0.5 ** 0.5)))


def _spatial_body(h_ref, g_ref, win_ref, bin_ref, lng_ref, lnb_ref, ws_ref, bs_ref, wout_ref,
                  o_ref, xn_ref, u_ref, v_ref, y_ref, *, c_dim):
    tm = h_ref.shape[0]
    groups, chunk, _ = ws_ref.shape
    gdim = c_dim // groups
    n_chunks = tm // chunk

    for n in range(tm // ROW_BLOCK):
        rows = slice(n * ROW_BLOCK, (n + 1) * ROW_BLOCK)
        xn_ref[rows, :] = _rms_norm(h_ref[rows, :], g_ref[...]).astype(BF16)
        v = _gelu(_dot(xn_ref[rows, :], win_ref[:, c_dim:2 * c_dim]) + bin_ref[:, c_dim:2 * c_dim])
        v_ref[rows, :] = _layer_norm(v, lng_ref[...], lnb_ref[...]).astype(BF16)
    for n in range(tm // ROW_BLOCK):
        rows = slice(n * ROW_BLOCK, (n + 1) * ROW_BLOCK)
        u_ref[rows, :] = _gelu(_dot(xn_ref[rows, :], win_ref[:, 0:c_dim]) + bin_ref[:, 0:c_dim])

    row = lax.broadcasted_iota(jnp.int32, (chunk, chunk), 0)
    col = lax.broadcasted_iota(jnp.int32, (chunk, chunk), 1)
    causal = col <= row
    sv = []
    for gi in range(groups):
        lanes = slice(gi * gdim, (gi + 1) * gdim)
        ws = jnp.where(causal, ws_ref[gi], 0.0).astype(BF16)
        vg = jnp.concatenate(
            [v_ref[n * chunk:(n + 1) * chunk, lanes] for n in range(n_chunks)], axis=1)
        sv.append(_dot(ws, vg))
    chunks_per_block = ROW_BLOCK // chunk
    for blk in range(tm // ROW_BLOCK):
        y = []
        for n in range(blk * chunks_per_block, (blk + 1) * chunks_per_block):
            gate = jnp.concatenate(
                [sv[gi][:, n * gdim:(n + 1) * gdim] + bs_ref[gi] for gi in range(groups)], axis=1)
            y.append((u_ref[n * chunk:(n + 1) * chunk, :] * gate).astype(BF16))
        rows = slice(blk * ROW_BLOCK, (blk + 1) * ROW_BLOCK)
        o_ref[rows, :] = h_ref[rows, :] + _dot(jnp.concatenate(y, axis=0), wout_ref[...])


def _spatial_gating(h, g, w_in, b_in, ln_v_g, ln_v_b, w_s, b_s, w_out, *, seq_len):
    t, d = h.shape
    c_dim = w_out.shape[0]
    groups, chunk, _ = w_s.shape
    gdim = c_dim // groups
    assert TOKEN_TILE % chunk == 0 and seq_len % TOKEN_TILE == 0
    bs = jnp.broadcast_to(b_s[:, :, None], (groups, chunk, gdim))
    args = (h, g.reshape(1, d), w_in, b_in.reshape(1, 2 * c_dim),
            ln_v_g.reshape(1, c_dim), ln_v_b.reshape(1, c_dim), w_s, bs, w_out)
    in_specs = [pl.BlockSpec((TOKEN_TILE, d), lambda i: (i, 0))]
    in_specs += [_resident(a.shape) for a in args[1:]]
    return pl.pallas_call(
        functools.partial(_spatial_body, c_dim=c_dim),
        grid=(t // TOKEN_TILE,),
        in_specs=in_specs,
        out_specs=pl.BlockSpec((TOKEN_TILE, d), lambda i: (i, 0)),
        out_shape=jax.ShapeDtypeStruct((t, d), F32),
        scratch_shapes=[
            pltpu.VMEM((TOKEN_TILE, d), BF16),
            pltpu.VMEM((TOKEN_TILE, c_dim), F32),
            pltpu.VMEM((TOKEN_TILE, c_dim), BF16),
            pltpu.VMEM((TOKEN_TILE, c_dim), BF16),
        ],
        compiler_params=pltpu.CompilerParams(
            dimension_semantics=("arbitrary",), vmem_limit_bytes=VMEM_LIMIT_BYTES),
        name="spatial_gating",
    )(*args)


def kernel(x, ev_norm_g, ev_w_in, ev_conv_a_w, ev_conv_a_b, ev_ln_a_g, ev_ln_a_b, ev_conv_b_w, ev_w_out, od_norm_g, od_w_in, od_b_in, od_ln_v_g, od_ln_v_b, od_w_s, od_b_s, od_w_out, mlp_norm_g, mlp_w1, mlp_w2, final_norm_g):
    bsz, s, d = x.shape
    depth = mlp_w1.shape[0]
    a_dim = ev_conv_a_w.shape[2]
    b_dim = ev_conv_b_w.shape[2]

    def mixer_weights(i):
        j = i // 2
        if i % 2 == 0:
            return (_conv_in_columns(ev_w_in[j], a_dim, b_dim), ev_w_out[j])
        return (od_w_in[j], od_w_out[j])

    def to_bf16(ws):
        return tuple(w.astype(BF16) for w in ws)

    mix_w = to_bf16(mixer_weights(0))
    mlp_w = None
    h = x.reshape(bsz * s, d)
    for i in range(depth):
        j = i // 2
        if i % 2 == 0:
            h, mlp_w = _conv_mix(h, ev_norm_g[j], mix_w[0], ev_conv_a_w[j], ev_conv_a_b[j],
                                 ev_ln_a_g[j], ev_ln_a_b[j], ev_conv_b_w[j], mix_w[1],
                                 seq_len=s, cast=(mlp_w1[i], mlp_w2[i]))
        else:
            h = _spatial_gating(h, od_norm_g[j], mix_w[0], od_b_in[j], od_ln_v_g[j], od_ln_v_b[j],
                                od_w_s[j], od_b_s[j], mix_w[1], seq_len=s)
            if mlp_w is None:
                mlp_w = to_bf16((mlp_w1[i], mlp_w2[i]))
        last = i == depth - 1
        cast = ()
        if not last:
            cast = mixer_weights(i + 1)
            if (i + 1) % 2 == 1:
                cast = cast + (mlp_w1[i + 1], mlp_w2[i + 1])
        h, staged = _mlp(h, mlp_norm_g[i], mlp_w[0], mlp_w[1], final_norm_g if last else None,
                         cast=cast)
        mix_w, mlp_w = staged[0:2], (staged[2:4] or None)
    return h.reshape(bsz, s, d)
```

```python
import functools
from typing import NamedTuple

import jax
import jax.numpy as jnp
from jax import lax
from jax.experimental import pallas as pl
from jax.experimental.pallas import tpu as pltpu

F32 = jnp.float32
BF16 = jnp.bfloat16

RMS_EPS = 1e-6
LN_EPS = 1e-5

SUBLANES = 8
LANES = 128
BF16_SUBLANES = 16
VMEM_LIMIT_BYTES = 56 * 1024 * 1024

TOKEN_TILE = 512
MLP_TOKEN_TILE = 1024
FF_CHUNK = 512
ROW_BLOCK = 256
CONV_ROWS = 32
GLU_COLS = 256
HIST_ROW_STRIDE = 2


def _rms_norm(x, g):
    ms = jnp.mean(x * x, axis=-1, keepdims=True)
    return x * lax.rsqrt(ms + RMS_EPS) * g


def _layer_norm(x, g, b):
    mu = jnp.mean(x, axis=-1, keepdims=True)
    xc = x - mu
    var = jnp.mean(xc * xc, axis=-1, keepdims=True)
    return xc * lax.rsqrt(var + LN_EPS) * g + b


def _dot(a, b):
    return jnp.dot(a, b, preferred_element_type=F32)


def _resident(shape):
    zeros = (0,) * len(shape)
    return pl.BlockSpec(shape, lambda *_: zeros, pipeline_mode=pl.Buffered(1))


def _halo(width):
    return -(-(width - 1) // SUBLANES) * SUBLANES


class _Layer(NamedTuple):
    stack: jax.Array
    index: int

    @property
    def shape(self):
        return self.stack.shape


def _cast_operands(weights):
    return [w.stack if isinstance(w, _Layer) else w for w in weights]


def _cast_stream(weights, n_steps):
    in_specs, out_specs, out_shapes = [], [], []
    for w in weights:
        rows, cols = w.shape[-2:]
        slab = rows // n_steps
        assert slab * n_steps == rows and slab % BF16_SUBLANES == 0
        if isinstance(w, _Layer):
            in_specs.append(pl.BlockSpec((None, slab, cols), lambda i, k=w.index: (k, i, 0)))
        else:
            in_specs.append(pl.BlockSpec((slab, cols), lambda i: (i, 0)))
        out_specs.append(pl.BlockSpec((slab, cols), lambda i: (i, 0)))
        out_shapes.append(jax.ShapeDtypeStruct((rows, cols), BF16))
    return in_specs, out_specs, out_shapes


def _cast_slabs(src_refs, dst_refs):
    for src_ref, dst_ref in zip(src_refs, dst_refs):
        dst_ref[...] = src_ref[...].astype(BF16)


def _mlp_body(h_ref, g_ref, w1_ref, w2_ref, *rest, final_norm, n_cast):
    if final_norm:
        gf_ref, rest = rest[0], rest[1:]
    cast_src, rest = rest[:n_cast], rest[n_cast:]
    o_ref, rest = rest[0], rest[1:]
    cast_dst, (xn_ref, acc_ref) = rest[:n_cast], rest[n_cast:]
    _cast_slabs(cast_src, cast_dst)
    xn_ref[...] = _rms_norm(h_ref[...], g_ref[...]).astype(BF16)
    fc = FF_CHUNK
    for c in range(w1_ref.shape[1] // fc):
        a = jnp.maximum(_dot(xn_ref[...], w1_ref[:, c * fc:(c + 1) * fc]), 0.0)
        upd = _dot((a * a).astype(BF16), w2_ref[c * fc:(c + 1) * fc, :])
        if c == 0:
            acc_ref[...] = upd
        else:
            acc_ref[...] += upd
    y = h_ref[...] + acc_ref[...]
    if final_norm:
        y = _rms_norm(y, gf_ref[...])
    o_ref[...] = y


def _mlp(h, g, w1, w2, final_g=None, cast=()):
    t, d = h.shape
    final_norm = final_g is not None
    n_steps = t // MLP_TOKEN_TILE
    args = [h, g.reshape(1, d), w1, w2]
    in_specs = [
        pl.BlockSpec((MLP_TOKEN_TILE, d), lambda i: (i, 0)),
        _resident((1, d)),
        _resident(w1.shape),
        _resident(w2.shape),
    ]
    if final_norm:
        args.append(final_g.reshape(1, d))
        in_specs.append(_resident((1, d)))
    cast_in, cast_out, cast_shapes = _cast_stream(cast, n_steps)
    outs = pl.pallas_call(
        functools.partial(_mlp_body, final_norm=final_norm, n_cast=len(cast)),
        grid=(n_steps,),
        in_specs=in_specs + cast_in,
        out_specs=[pl.BlockSpec((MLP_TOKEN_TILE, d), lambda i: (i, 0))] + cast_out,
        out_shape=[jax.ShapeDtypeStruct((t, d), F32)] + cast_shapes,
        scratch_shapes=[
            pltpu.VMEM((MLP_TOKEN_TILE, d), BF16),
            pltpu.VMEM((MLP_TOKEN_TILE, d), F32),
        ],
        compiler_params=pltpu.CompilerParams(
            dimension_semantics=("arbitrary",), vmem_limit_bytes=VMEM_LIMIT_BYTES),
        name="mlp_final" if final_norm else "mlp",
    )(*args, *_cast_operands(cast))
    return outs[0], tuple(outs[1:])


def _hist_rows(start, size):
    return pl.ds(HIST_ROW_STRIDE * start, size, stride=HIST_ROW_STRIDE)


def _hist_store(buf_ref, start, x, first_block=0):
    for j in range(x.shape[1] // LANES):
        buf_ref[first_block + j, _hist_rows(start, x.shape[0]), :] = x[:, j * LANES:(j + 1) * LANES]


def _hist_carry(buf_ref, halo, ts, sequence_end):
    for j in range(buf_ref.shape[0]):
        tail = buf_ref[j, _hist_rows(ts, halo), :]
        buf_ref[j, _hist_rows(0, halo), :] = jnp.where(sequence_end, 0.0, tail)


def _causal_conv(buf_ref, w_ref, halo, row0, rows, j):
    taps = w_ref.shape[0]
    acc = None
    for k in range(taps):
        start = halo + row0 - (taps - 1 - k)
        term = buf_ref[j, _hist_rows(start, rows), :] * w_ref[k:k + 1, j * LANES:(j + 1) * LANES]
        acc = term if acc is None else acc + term
    return acc


def _conv_mix_body(h_ref, g_ref, win_ref, caw_ref, cab_ref, lng_ref, lnb_ref, cbw_ref, wout_ref,
                   *rest, a_dim, b_dim, tiles_per_seq, n_cast):
    cast_src, rest = rest[:n_cast], rest[n_cast:]
    o_ref, rest = rest[0], rest[1:]
    cast_dst, (xn_ref, abuf_ref, cbuf_ref, ya_ref, mix_ref) = rest[:n_cast], rest[n_cast:]
    _cast_slabs(cast_src, cast_dst)
    ts = h_ref.shape[0]
    a_halo = _halo(caw_ref.shape[0])
    b_halo = _halo(cbw_ref.shape[0])
    s = pl.program_id(0)

    @pl.when(s == 0)
    def _():
        _hist_store(abuf_ref, 0, jnp.zeros((a_halo, a_dim), F32))
        _hist_store(cbuf_ref, 0, jnp.zeros((b_halo, b_dim), F32))

    xn_ref[...] = _rms_norm(h_ref[...], g_ref[...]).astype(BF16)

    n_conv_blocks = ts // CONV_ROWS
    z = _dot(xn_ref[...], win_ref[...])
    for p in range(a_dim // GLU_COLS):
        zp = z[:, 2 * p * GLU_COLS:2 * (p + 1) * GLU_COLS]
        glu = zp[:, 0:GLU_COLS] * jax.nn.sigmoid(zp[:, GLU_COLS:2 * GLU_COLS])
        _hist_store(abuf_ref, a_halo, glu, first_block=p * GLU_COLS // LANES)
    zb = z[:, 2 * a_dim:2 * a_dim + 2 * b_dim]
    _hist_store(cbuf_ref, b_halo, zb[:, 0:b_dim] * zb[:, b_dim:2 * b_dim])
    b_gate = z[:, 2 * a_dim + 2 * b_dim:2 * a_dim + 3 * b_dim]

    for j in range(a_dim // LANES):
        lanes = slice(j * LANES, (j + 1) * LANES)
        for blk in range(n_conv_blocks):
            rows = slice(blk * CONV_ROWS, (blk + 1) * CONV_ROWS)
            ya_ref[rows, lanes] = (
                _causal_conv(abuf_ref, caw_ref, a_halo, blk * CONV_ROWS, CONV_ROWS, j)
                + cab_ref[:, lanes])

    for blk in range(n_conv_blocks):
        rows = slice(blk * CONV_ROWS, (blk + 1) * CONV_ROWS)
        ya = _layer_norm(ya_ref[rows, :], lng_ref[...], lnb_ref[...])
        mix_ref[rows, 0:a_dim] = (ya * jax.nn.sigmoid(ya)).astype(BF16)
        yb = jnp.concatenate(
            [_causal_conv(cbuf_ref, cbw_ref, b_halo, blk * CONV_ROWS, CONV_ROWS, j)
             for j in range(b_dim // LANES)], axis=1)
        mix_ref[rows, a_dim:a_dim + b_dim] = (b_gate[rows, :] * yb).astype(BF16)

    for blk in range(ts // ROW_BLOCK):
        rows = slice(blk * ROW_BLOCK, (blk + 1) * ROW_BLOCK)
        o_ref[rows, :] = h_ref[rows, :] + _dot(mix_ref[rows, :], wout_ref[...])

    sequence_end = (s + 1) % tiles_per_seq == 0
    _hist_carry(abuf_ref, a_halo, ts, sequence_end)
    _hist_carry(cbuf_ref, b_halo, ts, sequence_end)


def _conv_in_columns(w_in, a_dim, b_dim):
    cols = []
    for p in range(a_dim // GLU_COLS):
        cols.append(w_in[:, p * GLU_COLS:(p + 1) * GLU_COLS])
        cols.append(w_in[:, a_dim + p * GLU_COLS:a_dim + (p + 1) * GLU_COLS])
    cols.append(w_in[:, 2 * a_dim + b_dim:2 * a_dim + 3 * b_dim])
    cols.append(w_in[:, 2 * a_dim:2 * a_dim + b_dim])
    return jnp.concatenate(cols, axis=1)


def _conv_mix(h, g, w_in_cols, conv_a_w, conv_a_b, ln_a_g, ln_a_b, conv_b_w, w_out, *, seq_len,
              cast=()):
    t, d = h.shape
    ka, a_dim = conv_a_w.shape
    kb, b_dim = conv_b_w.shape
    a_halo = _halo(ka)
    b_halo = _halo(kb)
    ts = TOKEN_TILE
    assert seq_len % ts == 0 and t % seq_len == 0
    n_steps = t // ts
    weights = (g.reshape(1, d), w_in_cols, conv_a_w, conv_a_b.reshape(1, a_dim),
               ln_a_g.reshape(1, a_dim), ln_a_b.reshape(1, a_dim), conv_b_w, w_out)
    in_specs = [pl.BlockSpec((ts, d), lambda s: (s, 0))]
    in_specs += [_resident(a.shape) for a in weights]
    cast_in, cast_out, cast_shapes = _cast_stream(cast, n_steps)
    outs = pl.pallas_call(
        functools.partial(_conv_mix_body, a_dim=a_dim, b_dim=b_dim, tiles_per_seq=seq_len // ts,
                          n_cast=len(cast)),
        grid=(n_steps,),
        in_specs=in_specs + cast_in,
        out_specs=[pl.BlockSpec((ts, d), lambda s: (s, 0))] + cast_out,
        out_shape=[jax.ShapeDtypeStruct((t, d), F32)] + cast_shapes,
        scratch_shapes=[
            pltpu.VMEM((ts, d), BF16),
            pltpu.VMEM((a_dim // LANES, HIST_ROW_STRIDE * (a_halo + ts), LANES), F32),
            pltpu.VMEM((b_dim // LANES, HIST_ROW_STRIDE * (b_halo + ts), LANES), F32),
            pltpu.VMEM((ts, a_dim), F32),
            pltpu.VMEM((ts, a_dim + b_dim), BF16),
        ],
        compiler_params=pltpu.CompilerParams(
            dimension_semantics=("arbitrary",), vmem_limit_bytes=VMEM_LIMIT_BYTES),
        name="conv_mix",
    )(h, *weights, *_cast_operands(cast))
    return outs[0], tuple(outs[1:])


def _gelu(x):
    return 0.5 * x * (1.0 + lax.erf(x * (0.5 ** 0.5)))


def _spatial_body(h_ref, g_ref, win_ref, bin_ref, lng_ref, lnb_ref, ws_ref, bs_ref, wout_ref,
                  o_ref, xn_ref, u_ref, v_ref, y_ref, *, c_dim):
    tm = h_ref.shape[0]
    groups, chunk, _ = ws_ref.shape
    gdim = c_dim // groups
    n_chunks = tm // chunk

    for n in range(tm // ROW_BLOCK):
        rows = slice(n * ROW_BLOCK, (n + 1) * ROW_BLOCK)
        xn_ref[rows, :] = _rms_norm(h_ref[rows, :], g_ref[...]).astype(BF16)
        v = _gelu(_dot(xn_ref[rows, :], win_ref[:, c_dim:2 * c_dim]) + bin_ref[:, c_dim:2 * c_dim])
        v_ref[rows, :] = _layer_norm(v, lng_ref[...], lnb_ref[...]).astype(BF16)
    for n in range(tm // ROW_BLOCK):
        rows = slice(n * ROW_BLOCK, (n + 1) * ROW_BLOCK)
        u_ref[rows, :] = _gelu(_dot(xn_ref[rows, :], win_ref[:, 0:c_dim]) + bin_ref[:, 0:c_dim])

    row = lax.broadcasted_iota(jnp.int32, (chunk, chunk), 0)
    col = lax.broadcasted_iota(jnp.int32, (chunk, chunk), 1)
    causal = col <= row
    sv = []
    for gi in range(groups):
        lanes = slice(gi * gdim, (gi + 1) * gdim)
        ws = jnp.where(causal, ws_ref[gi], 0.0).astype(BF16)
        vg = jnp.concatenate(
            [v_ref[n * chunk:(n + 1) * chunk, lanes] for n in range(n_chunks)], axis=1)
        sv.append(_dot(ws, vg))
    chunks_per_block = ROW_BLOCK // chunk
    for blk in range(tm // ROW_BLOCK):
        y = []
        for n in range(blk * chunks_per_block, (blk + 1) * chunks_per_block):
            gate = jnp.concatenate(
                [sv[gi][:, n * gdim:(n + 1) * gdim] + bs_ref[gi] for gi in range(groups)], axis=1)
            y.append((u_ref[n * chunk:(n + 1) * chunk, :] * gate).astype(BF16))
        rows = slice(blk * ROW_BLOCK, (blk + 1) * ROW_BLOCK)
        o_ref[rows, :] = h_ref[rows, :] + _dot(jnp.concatenate(y, axis=0), wout_ref[...])


def _spatial_gating(h, g, w_in, b_in, ln_v_g, ln_v_b, w_s, b_s, w_out, *, seq_len):
    t, d = h.shape
    c_dim = w_out.shape[0]
    groups, chunk, _ = w_s.shape
    gdim = c_dim // groups
    assert TOKEN_TILE % chunk == 0 and seq_len % TOKEN_TILE == 0
    bs = jnp.broadcast_to(b_s[:, :, None], (groups, chunk, gdim))
    args = (h, g.reshape(1, d), w_in, b_in.reshape(1, 2 * c_dim),
            ln_v_g.reshape(1, c_dim), ln_v_b.reshape(1, c_dim), w_s, bs, w_out)
    in_specs = [pl.BlockSpec((TOKEN_TILE, d), lambda i: (i, 0))]
    in_specs += [_resident(a.shape) for a in args[1:]]
    return pl.pallas_call(
        functools.partial(_spatial_body, c_dim=c_dim),
        grid=(t // TOKEN_TILE,),
        in_specs=in_specs,
        out_specs=pl.BlockSpec((TOKEN_TILE, d), lambda i: (i, 0)),
        out_shape=jax.ShapeDtypeStruct((t, d), F32),
        scratch_shapes=[
            pltpu.VMEM((TOKEN_TILE, d), BF16),
            pltpu.VMEM((TOKEN_TILE, c_dim), F32),
            pltpu.VMEM((TOKEN_TILE, c_dim), BF16),
            pltpu.VMEM((TOKEN_TILE, c_dim), BF16),
        ],
        compiler_params=pltpu.CompilerParams(
            dimension_semantics=("arbitrary",), vmem_limit_bytes=VMEM_LIMIT_BYTES),
        name="spatial_gating",
    )(*args)


def kernel(x, ev_norm_g, ev_w_in, ev_conv_a_w, ev_conv_a_b, ev_ln_a_g, ev_ln_a_b, ev_conv_b_w, ev_w_out, od_norm_g, od_w_in, od_b_in, od_ln_v_g, od_ln_v_b, od_w_s, od_b_s, od_w_out, mlp_norm_g, mlp_w1, mlp_w2, final_norm_g):
    bsz, s, d = x.shape
    depth = mlp_w1.shape[0]
    a_dim = ev_conv_a_w.shape[2]
    b_dim = ev_conv_b_w.shape[2]

    def layer(stack, k, stream):
        return _Layer(stack, k) if stream else stack[k]

    def mixer_weights(i, stream):
        j = i // 2
        if i % 2 == 0:
            return (_conv_in_columns(ev_w_in[j], a_dim, b_dim), layer(ev_w_out, j, stream))
        return (layer(od_w_in, j, stream), layer(od_w_out, j, stream))

    def to_bf16(ws):
        return tuple(w.astype(BF16) for w in ws)

    mix_w = to_bf16(mixer_weights(0, stream=False))
    mlp_w = None
    h = x.reshape(bsz * s, d)
    for i in range(depth):
        j = i // 2
        if i % 2 == 0:
            h, mlp_w = _conv_mix(h, ev_norm_g[j], mix_w[0], ev_conv_a_w[j], ev_conv_a_b[j],
                                 ev_ln_a_g[j], ev_ln_a_b[j], ev_conv_b_w[j], mix_w[1],
                                 seq_len=s, cast=(_Layer(mlp_w1, i), _Layer(mlp_w2, i)))
        else:
            h = _spatial_gating(h, od_norm_g[j], mix_w[0], od_b_in[j], od_ln_v_g[j], od_ln_v_b[j],
                                od_w_s[j], od_b_s[j], mix_w[1], seq_len=s)
            if mlp_w is None:
                mlp_w = to_bf16((mlp_w1[i], mlp_w2[i]))
        last = i == depth - 1
        cast = ()
        if not last:
            cast = mixer_weights(i + 1, stream=True)
            if (i + 1) % 2 == 1:
                cast = cast + (_Layer(mlp_w1, i + 1), _Layer(mlp_w2, i + 1))
        h, staged = _mlp(h, mlp_norm_g[i], mlp_w[0], mlp_w[1], final_norm_g if last else None,
                         cast=cast)
        mix_w, mlp_w = staged[0:2], (staged[2:4] or None)
    return h.reshape(bsz, s, d)
```

```python
import functools
from typing import NamedTuple

import jax
import jax.numpy as jnp
from jax import lax
from jax.experimental import pallas as pl
from jax.experimental.pallas import tpu as pltpu

F32 = jnp.float32
BF16 = jnp.bfloat16

RMS_EPS = 1e-6
LN_EPS = 1e-5

SUBLANES = 8
LANES = 128
BF16_SUBLANES = 16
VMEM_LIMIT_BYTES = 56 * 1024 * 1024

TOKEN_TILE = 1024
MLP_TOKEN_TILE = 1024
FF_CHUNK = 512
ROW_BLOCK = 256
CONV_ROWS = 32
GLU_COLS = 256
HIST_ROW_STRIDE = 2


def _rms_norm(x, g):
    ms = jnp.mean(x * x, axis=-1, keepdims=True)
    return x * lax.rsqrt(ms + RMS_EPS) * g


def _layer_norm(x, g, b):
    mu = jnp.mean(x, axis=-1, keepdims=True)
    xc = x - mu
    var = jnp.mean(xc * xc, axis=-1, keepdims=True)
    return xc * lax.rsqrt(var + LN_EPS) * g + b


def _dot(a, b):
    return jnp.dot(a, b, preferred_element_type=F32)


def _resident(shape):
    zeros = (0,) * len(shape)
    return pl.BlockSpec(shape, lambda *_: zeros, pipeline_mode=pl.Buffered(1))


def _halo(width):
    return -(-(width - 1) // SUBLANES) * SUBLANES


class _Layer(NamedTuple):
    stack: jax.Array
    index: int

    @property
    def shape(self):
        return self.stack.shape


def _cast_operands(weights):
    return [w.stack if isinstance(w, _Layer) else w for w in weights]


def _cast_stream(weights, n_steps):
    in_specs, out_specs, out_shapes = [], [], []
    for w in weights:
        rows, cols = w.shape[-2:]
        slab = rows // n_steps
        assert slab * n_steps == rows and slab % BF16_SUBLANES == 0
        if isinstance(w, _Layer):
            in_specs.append(pl.BlockSpec((None, slab, cols), lambda i, k=w.index: (k, i, 0)))
        else:
            in_specs.append(pl.BlockSpec((slab, cols), lambda i: (i, 0)))
        out_specs.append(pl.BlockSpec((slab, cols), lambda i: (i, 0)))
        out_shapes.append(jax.ShapeDtypeStruct((rows, cols), BF16))
    return in_specs, out_specs, out_shapes


def _cast_slabs(src_refs, dst_refs):
    for src_ref, dst_ref in zip(src_refs, dst_refs):
        dst_ref[...] = src_ref[...].astype(BF16)


def _mlp_body(h_ref, g_ref, w1_ref, w2_ref, *rest, final_norm, n_cast):
    if final_norm:
        gf_ref, rest = rest[0], rest[1:]
    cast_src, rest = rest[:n_cast], rest[n_cast:]
    o_ref, rest = rest[0], rest[1:]
    cast_dst, (xn_ref, acc_ref) = rest[:n_cast], rest[n_cast:]
    _cast_slabs(cast_src, cast_dst)
    xn_ref[...] = _rms_norm(h_ref[...], g_ref[...]).astype(BF16)
    fc = FF_CHUNK
    for c in range(w1_ref.shape[1] // fc):
        a = jnp.maximum(_dot(xn_ref[...], w1_ref[:, c * fc:(c + 1) * fc]), 0.0)
        upd = _dot((a * a).astype(BF16), w2_ref[c * fc:(c + 1) * fc, :])
        if c == 0:
            acc_ref[...] = upd
        else:
            acc_ref[...] += upd
    y = h_ref[...] + acc_ref[...]
    if final_norm:
        y = _rms_norm(y, gf_ref[...])
    o_ref[...] = y


def _mlp(h, g, w1, w2, final_g=None, cast=()):
    t, d = h.shape
    final_norm = final_g is not None
    n_steps = t // MLP_TOKEN_TILE
    args = [h, g.reshape(1, d), w1, w2]
    in_specs = [
        pl.BlockSpec((MLP_TOKEN_TILE, d), lambda i: (i, 0)),
        _resident((1, d)),
        _resident(w1.shape),
        _resident(w2.shape),
    ]
    if final_norm:
        args.append(final_g.reshape(1, d))
        in_specs.append(_resident((1, d)))
    cast_in, cast_out, cast_shapes = _cast_stream(cast, n_steps)
    outs = pl.pallas_call(
        functools.partial(_mlp_body, final_norm=final_norm, n_cast=len(cast)),
        grid=(n_steps,),
        in_specs=in_specs + cast_in,
        out_specs=[pl.BlockSpec((MLP_TOKEN_TILE, d), lambda i: (i, 0))] + cast_out,
        out_shape=[jax.ShapeDtypeStruct((t, d), F32)] + cast_shapes,
        scratch_shapes=[
            pltpu.VMEM((MLP_TOKEN_TILE, d), BF16),
            pltpu.VMEM((MLP_TOKEN_TILE, d), F32),
        ],
        compiler_params=pltpu.CompilerParams(
            dimension_semantics=("arbitrary",), vmem_limit_bytes=VMEM_LIMIT_BYTES),
        name="mlp_final" if final_norm else "mlp",
    )(*args, *_cast_operands(cast))
    return outs[0], tuple(outs[1:])


def _hist_rows(start, size):
    return pl.ds(HIST_ROW_STRIDE * start, size, stride=HIST_ROW_STRIDE)


def _hist_store(buf_ref, start, x, first_block=0):
    for j in range(x.shape[1] // LANES):
        buf_ref[first_block + j, _hist_rows(start, x.shape[0]), :] = x[:, j * LANES:(j + 1) * LANES]


def _hist_carry(buf_ref, halo, ts, sequence_end):
    for j in range(buf_ref.shape[0]):
        tail = buf_ref[j, _hist_rows(ts, halo), :]
        buf_ref[j, _hist_rows(0, halo), :] = jnp.where(sequence_end, 0.0, tail)


def _causal_conv(buf_ref, w_ref, halo, row0, rows, j):
    taps = w_ref.shape[0]
    acc = None
    for k in range(taps):
        start = halo + row0 - (taps - 1 - k)
        term = buf_ref[j, _hist_rows(start, rows), :] * w_ref[k:k + 1, j * LANES:(j + 1) * LANES]
        acc = term if acc is None else acc + term
    return acc


def _conv_mix_body(h_ref, g_ref, win_ref, caw_ref, cab_ref, lng_ref, lnb_ref, cbw_ref, wout_ref,
                   *rest, a_dim, b_dim, tiles_per_seq, n_cast):
    cast_src, rest = rest[:n_cast], rest[n_cast:]
    o_ref, rest = rest[0], rest[1:]
    cast_dst, (xn_ref, abuf_ref, cbuf_ref, ya_ref, mix_ref) = rest[:n_cast], rest[n_cast:]
    _cast_slabs(cast_src, cast_dst)
    ts = h_ref.shape[0]
    a_halo = _halo(caw_ref.shape[0])
    b_halo = _halo(cbw_ref.shape[0])
    s = pl.program_id(0)

    @pl.when(s == 0)
    def _():
        _hist_store(abuf_ref, 0, jnp.zeros((a_halo, a_dim), F32))
        _hist_store(cbuf_ref, 0, jnp.zeros((b_halo, b_dim), F32))

    xn_ref[...] = _rms_norm(h_ref[...], g_ref[...]).astype(BF16)

    n_conv_blocks = ts // CONV_ROWS
    z = _dot(xn_ref[...], win_ref[...])
    for p in range(a_dim // GLU_COLS):
        zp = z[:, 2 * p * GLU_COLS:2 * (p + 1) * GLU_COLS]
        glu = zp[:, 0:GLU_COLS] * jax.nn.sigmoid(zp[:, GLU_COLS:2 * GLU_COLS])
        _hist_store(abuf_ref, a_halo, glu, first_block=p * GLU_COLS // LANES)
    zb = z[:, 2 * a_dim:2 * a_dim + 2 * b_dim]
    _hist_store(cbuf_ref, b_halo, zb[:, 0:b_dim] * zb[:, b_dim:2 * b_dim])
    b_gate = z[:, 2 * a_dim + 2 * b_dim:2 * a_dim + 3 * b_dim]

    for j in range(a_dim // LANES):
        lanes = slice(j * LANES, (j + 1) * LANES)
        for blk in range(n_conv_blocks):
            rows = slice(blk * CONV_ROWS, (blk + 1) * CONV_ROWS)
            ya_ref[rows, lanes] = (
                _causal_conv(abuf_ref, caw_ref, a_halo, blk * CONV_ROWS, CONV_ROWS, j)
                + cab_ref[:, lanes])

    for blk in range(n_conv_blocks):
        rows = slice(blk * CONV_ROWS, (blk + 1) * CONV_ROWS)
        ya = _layer_norm(ya_ref[rows, :], lng_ref[...], lnb_ref[...])
        mix_ref[rows, 0:a_dim] = (ya * jax.nn.sigmoid(ya)).astype(BF16)
        yb = jnp.concatenate(
            [_causal_conv(cbuf_ref, cbw_ref, b_halo, blk * CONV_ROWS, CONV_ROWS, j)
             for j in range(b_dim // LANES)], axis=1)
        mix_ref[rows, a_dim:a_dim + b_dim] = (b_gate[rows, :] * yb).astype(BF16)

    for blk in range(ts // ROW_BLOCK):
        rows = slice(blk * ROW_BLOCK, (blk + 1) * ROW_BLOCK)
        o_ref[rows, :] = h_ref[rows, :] + _dot(mix_ref[rows, :], wout_ref[...])

    sequence_end = (s + 1) % tiles_per_seq == 0
    _hist_carry(abuf_ref, a_halo, ts, sequence_end)
    _hist_carry(cbuf_ref, b_halo, ts, sequence_end)


def _conv_in_columns(w_in, a_dim, b_dim):
    cols = []
    for p in range(a_dim // GLU_COLS):
        cols.append(w_in[:, p * GLU_COLS:(p + 1) * GLU_COLS])
        cols.append(w_in[:, a_dim + p * GLU_COLS:a_dim + (p + 1) * GLU_COLS])
    cols.append(w_in[:, 2 * a_dim + b_dim:2 * a_dim + 3 * b_dim])
    cols.append(w_in[:, 2 * a_dim:2 * a_dim + b_dim])
    return jnp.concatenate(cols, axis=1)


def _conv_mix(h, g, w_in_cols, conv_a_w, conv_a_b, ln_a_g, ln_a_b, conv_b_w, w_out, *, seq_len,
              cast=()):
    t, d = h.shape
    ka, a_dim = conv_a_w.shape
    kb, b_dim = conv_b_w.shape
    a_halo = _halo(ka)
    b_halo = _halo(kb)
    ts = TOKEN_TILE
    assert seq_len % ts == 0 and t % seq_len == 0
    n_steps = t // ts
    weights = (g.reshape(1, d), w_in_cols, conv_a_w, conv_a_b.reshape(1, a_dim),
               ln_a_g.reshape(1, a_dim), ln_a_b.reshape(1, a_dim), conv_b_w, w_out)
    in_specs = [pl.BlockSpec((ts, d), lambda s: (s, 0))]
    in_specs += [_resident(a.shape) for a in weights]
    cast_in, cast_out, cast_shapes = _cast_stream(cast, n_steps)
    outs = pl.pallas_call(
        functools.partial(_conv_mix_body, a_dim=a_dim, b_dim=b_dim, tiles_per_seq=seq_len // ts,
                          n_cast=len(cast)),
        grid=(n_steps,),
        in_specs=in_specs + cast_in,
        out_specs=[pl.BlockSpec((ts, d), lambda s: (s, 0))] + cast_out,
        out_shape=[jax.ShapeDtypeStruct((t, d), F32)] + cast_shapes,
        scratch_shapes=[
            pltpu.VMEM((ts, d), BF16),
            pltpu.VMEM((a_dim // LANES, HIST_ROW_STRIDE * (a_halo + ts), LANES), F32),
            pltpu.VMEM((b_dim // LANES, HIST_ROW_STRIDE * (b_halo + ts), LANES), F32),
            pltpu.VMEM((ts, a_dim), F32),
            pltpu.VMEM((ts, a_dim + b_dim), BF16),
        ],
        compiler_params=pltpu.CompilerParams(
            dimension_semantics=("arbitrary",), vmem_limit_bytes=VMEM_LIMIT_BYTES),
        name="conv_mix",
    )(h, *weights, *_cast_operands(cast))
    return outs[0], tuple(outs[1:])


def _gelu(x):
    return 0.5 * x * (1.0 + lax.erf(x * (0.5 ** 0.5)))


def _spatial_body(h_ref, g_ref, win_ref, bin_ref, lng_ref, lnb_ref, ws_ref, bs_ref, wout_ref,
                  o_ref, xn_ref, u_ref, v_ref, y_ref, *, c_dim):
    tm = h_ref.shape[0]
    groups, chunk, _ = ws_ref.shape
    gdim = c_dim // groups
    n_chunks = tm // chunk

    for n in range(tm // ROW_BLOCK):
        rows = slice(n * ROW_BLOCK, (n + 1) * ROW_BLOCK)
        xn_ref[rows, :] = _rms_norm(h_ref[rows, :], g_ref[...]).astype(BF16)
        v = _gelu(_dot(xn_ref[rows, :], win_ref[:, c_dim:2 * c_dim]) + bin_ref[:, c_dim:2 * c_dim])
        v_ref[rows, :] = _layer_norm(v, lng_ref[...], lnb_ref[...]).astype(BF16)
    for n in range(tm // ROW_BLOCK):
        rows = slice(n * ROW_BLOCK, (n + 1) * ROW_BLOCK)
        u_ref[rows, :] = _gelu(_dot(xn_ref[rows, :], win_ref[:, 0:c_dim]) + bin_ref[:, 0:c_dim])

    row = lax.broadcasted_iota(jnp.int32, (chunk, chunk), 0)
    col = lax.broadcasted_iota(jnp.int32, (chunk, chunk), 1)
    causal = col <= row
    sv = []
    for gi in range(groups):
        lanes = slice(gi * gdim, (gi + 1) * gdim)
        ws = jnp.where(causal, ws_ref[gi], 0.0).astype(BF16)
        vg = jnp.concatenate(
            [v_ref[n * chunk:(n + 1) * chunk, lanes] for n in range(n_chunks)], axis=1)
        sv.append(_dot(ws, vg))
    chunks_per_block = ROW_BLOCK // chunk
    for blk in range(tm // ROW_BLOCK):
        y = []
        for n in range(blk * chunks_per_block, (blk + 1) * chunks_per_block):
            gate = jnp.concatenate(
                [sv[gi][:, n * gdim:(n + 1) * gdim] + bs_ref[gi] for gi in range(groups)], axis=1)
            y.append((u_ref[n * chunk:(n + 1) * chunk, :] * gate).astype(BF16))
        rows = slice(blk * ROW_BLOCK, (blk + 1) * ROW_BLOCK)
        o_ref[rows, :] = h_ref[rows, :] + _dot(jnp.concatenate(y, axis=0), wout_ref[...])


def _spatial_gating(h, g, w_in, b_in, ln_v_g, ln_v_b, w_s, b_s, w_out, *, seq_len):
    t, d = h.shape
    c_dim = w_out.shape[0]
    groups, chunk, _ = w_s.shape
    gdim = c_dim // groups
    assert TOKEN_TILE % chunk == 0 and seq_len % TOKEN_TILE == 0
    bs = jnp.broadcast_to(b_s[:, :, None], (groups, chunk, gdim))
    args = (h, g.reshape(1, d), w_in, b_in.reshape(1, 2 * c_dim),
            ln_v_g.reshape(1, c_dim), ln_v_b.reshape(1, c_dim), w_s, bs, w_out)
    in_specs = [pl.BlockSpec((TOKEN_TILE, d), lambda i: (i, 0))]
    in_specs += [_resident(a.shape) for a in args[1:]]
    return pl.pallas_call(
        functools.partial(_spatial_body, c_dim=c_dim),
        grid=(t // TOKEN_TILE,),
        in_specs=in_specs,
        out_specs=pl.BlockSpec((TOKEN_TILE, d), lambda i: (i, 0)),
        out_shape=jax.ShapeDtypeStruct((t, d), F32),
        scratch_shapes=[
            pltpu.VMEM((TOKEN_TILE, d), BF16),
            pltpu.VMEM((TOKEN_TILE, c_dim), F32),
            pltpu.VMEM((TOKEN_TILE, c_dim), BF16),
            pltpu.VMEM((TOKEN_TILE, c_dim), BF16),
        ],
        compiler_params=pltpu.CompilerParams(
            dimension_semantics=("arbitrary",), vmem_limit_bytes=VMEM_LIMIT_BYTES),
        name="spatial_gating",
    )(*args)


def kernel(x, ev_norm_g, ev_w_in, ev_conv_a_w, ev_conv_a_b, ev_ln_a_g, ev_ln_a_b, ev_conv_b_w, ev_w_out, od_norm_g, od_w_in, od_b_in, od_ln_v_g, od_ln_v_b, od_w_s, od_b_s, od_w_out, mlp_norm_g, mlp_w1, mlp_w2, final_norm_g):
    bsz, s, d = x.shape
    depth = mlp_w1.shape[0]
    a_dim = ev_conv_a_w.shape[2]
    b_dim = ev_conv_b_w.shape[2]

    def layer(stack, k, stream):
        return _Layer(stack, k) if stream else stack[k]

    def mixer_weights(i, stream):
        j = i // 2
        if i % 2 == 0:
            return (_conv_in_columns(ev_w_in[j], a_dim, b_dim), layer(ev_w_out, j, stream))
        return (layer(od_w_in, j, stream), layer(od_w_out, j, stream))

    def to_bf16(ws):
        return tuple(w.astype(BF16) for w in ws)

    mix_w = to_bf16(mixer_weights(0, stream=False))
    mlp_w = None
    h = x.reshape(bsz * s, d)
    for i in range(depth):
        j = i // 2
        if i % 2 == 0:
            h, mlp_w = _conv_mix(h, ev_norm_g[j], mix_w[0], ev_conv_a_w[j], ev_conv_a_b[j],
                                 ev_ln_a_g[j], ev_ln_a_b[j], ev_conv_b_w[j], mix_w[1],
                                 seq_len=s, cast=(_Layer(mlp_w1, i), _Layer(mlp_w2, i)))
        else:
            h = _spatial_gating(h, od_norm_g[j], mix_w[0], od_b_in[j], od_ln_v_g[j], od_ln_v_b[j],
                                od_w_s[j], od_b_s[j], mix_w[1], seq_len=s)
            if mlp_w is None:
                mlp_w = to_bf16((mlp_w1[i], mlp_w2[i]))
        last = i == depth - 1
        cast = ()
        if not last:
            cast = mixer_weights(i + 1, stream=True)
            if (i + 1) % 2 == 1:
                cast = cast + (_Layer(mlp_w1, i + 1), _Layer(mlp_w2, i + 1))
        h, staged = _mlp(h, mlp_norm_g[i], mlp_w[0], mlp_w[1], final_norm_g if last else None,
                         cast=cast)
        mix_w, mlp_w = staged[0:2], (staged[2:4] or None)
    return h.reshape(bsz, s, d)
```

```python
import functools
from typing import NamedTuple

import jax
import jax.numpy as jnp
from jax import lax
from jax.experimental import pallas as pl
from jax.experimental.pallas import tpu as pltpu

F32 = jnp.float32
BF16 = jnp.bfloat16

RMS_EPS = 1e-6
LN_EPS = 1e-5

SUBLANES = 8
LANES = 128
BF16_SUBLANES = 16
VMEM_LIMIT_BYTES = 56 * 1024 * 1024

TOKEN_TILE = 1024
MLP_TOKEN_TILE = 1024
FF_CHUNK = 512
ROW_BLOCK = 256
CONV_ROWS = 32
GLU_COLS = 256
HIST_ROW_STRIDE = 2


def _rms_norm(x, g):
    ms = jnp.mean(x * x, axis=-1, keepdims=True)
    return x * lax.rsqrt(ms + RMS_EPS) * g


def _layer_norm(x, g, b):
    mu = jnp.mean(x, axis=-1, keepdims=True)
    xc = x - mu
    var = jnp.mean(xc * xc, axis=-1, keepdims=True)
    return xc * lax.rsqrt(var + LN_EPS) * g + b


def _gated(y, x):
    hy = 0.5 * y
    return hy + hy * jnp.tanh(0.5 * x)


def _dot(a, b):
    return jnp.dot(a, b, preferred_element_type=F32)


def _resident(shape):
    zeros = (0,) * len(shape)
    return pl.BlockSpec(shape, lambda *_: zeros, pipeline_mode=pl.Buffered(1))


def _halo(width):
    return -(-(width - 1) // SUBLANES) * SUBLANES


class _Layer(NamedTuple):
    stack: jax.Array
    index: int

    @property
    def shape(self):
        return self.stack.shape


def _cast_operands(weights):
    return [w.stack if isinstance(w, _Layer) else w for w in weights]


def _cast_stream(weights, n_steps):
    in_specs, out_specs, out_shapes = [], [], []
    for w in weights:
        rows, cols = w.shape[-2:]
        slab = rows // n_steps
        assert slab * n_steps == rows and slab % BF16_SUBLANES == 0
        if isinstance(w, _Layer):
            in_specs.append(pl.BlockSpec((None, slab, cols), lambda i, k=w.index: (k, i, 0)))
        else:
            in_specs.append(pl.BlockSpec((slab, cols), lambda i: (i, 0)))
        out_specs.append(pl.BlockSpec((slab, cols), lambda i: (i, 0)))
        out_shapes.append(jax.ShapeDtypeStruct((rows, cols), BF16))
    return in_specs, out_specs, out_shapes


def _cast_slabs(src_refs, dst_refs):
    for src_ref, dst_ref in zip(src_refs, dst_refs):
        dst_ref[...] = src_ref[...].astype(BF16)


def _mlp_body(h_ref, g_ref, w1_ref, w2_ref, *rest, final_norm, n_cast):
    if final_norm:
        gf_ref, rest = rest[0], rest[1:]
    cast_src, rest = rest[:n_cast], rest[n_cast:]
    o_ref, rest = rest[0], rest[1:]
    cast_dst, (xn_ref, acc_ref) = rest[:n_cast], rest[n_cast:]
    _cast_slabs(cast_src, cast_dst)
    xn_ref[...] = _rms_norm(h_ref[...], g_ref[...]).astype(BF16)
    fc = FF_CHUNK
    for c in range(w1_ref.shape[1] // fc):
        a = jnp.maximum(_dot(xn_ref[...], w1_ref[:, c * fc:(c + 1) * fc]), 0.0)
        upd = _dot((a * a).astype(BF16), w2_ref[c * fc:(c + 1) * fc, :])
        if c == 0:
            acc_ref[...] = upd
        else:
            acc_ref[...] += upd
    y = h_ref[...] + acc_ref[...]
    if final_norm:
        y = _rms_norm(y, gf_ref[...])
    o_ref[...] = y


def _mlp(h, g, w1, w2, final_g=None, cast=()):
    t, d = h.shape
    final_norm = final_g is not None
    n_steps = t // MLP_TOKEN_TILE
    args = [h, g.reshape(1, d), w1, w2]
    in_specs = [
        pl.BlockSpec((MLP_TOKEN_TILE, d), lambda i: (i, 0)),
        _resident((1, d)),
        _resident(w1.shape),
        _resident(w2.shape),
    ]
    if final_norm:
        args.append(final_g.reshape(1, d))
        in_specs.append(_resident((1, d)))
    cast_in, cast_out, cast_shapes = _cast_stream(cast, n_steps)
    outs = pl.pallas_call(
        functools.partial(_mlp_body, final_norm=final_norm, n_cast=len(cast)),
        grid=(n_steps,),
        in_specs=in_specs + cast_in,
        out_specs=[pl.BlockSpec((MLP_TOKEN_TILE, d), lambda i: (i, 0))] + cast_out,
        out_shape=[jax.ShapeDtypeStruct((t, d), F32)] + cast_shapes,
        scratch_shapes=[
            pltpu.VMEM((MLP_TOKEN_TILE, d), BF16),
            pltpu.VMEM((MLP_TOKEN_TILE, d), F32),
        ],
        compiler_params=pltpu.CompilerParams(
            dimension_semantics=("arbitrary",), vmem_limit_bytes=VMEM_LIMIT_BYTES),
        name="mlp_final" if final_norm else "mlp",
    )(*args, *_cast_operands(cast))
    return outs[0], tuple(outs[1:])


def _hist_rows(start, size):
    return pl.ds(HIST_ROW_STRIDE * start, size, stride=HIST_ROW_STRIDE)


def _hist_store(buf_ref, start, x, first_block=0):
    for j in range(x.shape[1] // LANES):
        buf_ref[first_block + j, _hist_rows(start, x.shape[0]), :] = x[:, j * LANES:(j + 1) * LANES]


def _hist_carry(buf_ref, halo, ts, sequence_end):
    for j in range(buf_ref.shape[0]):
        tail = buf_ref[j, _hist_rows(ts, halo), :]
        buf_ref[j, _hist_rows(0, halo), :] = jnp.where(sequence_end, 0.0, tail)


def _causal_conv(buf_ref, w_ref, halo, row0, rows, j):
    taps = w_ref.shape[0]
    acc = None
    for k in range(taps):
        start = halo + row0 - (taps - 1 - k)
        term = buf_ref[j, _hist_rows(start, rows), :] * w_ref[k:k + 1, j * LANES:(j + 1) * LANES]
        acc = term if acc is None else acc + term
    return acc


def _conv_mix_body(h_ref, g_ref, win_ref, caw_ref, cab_ref, lng_ref, lnb_ref, cbw_ref, wout_ref,
                   *rest, a_dim, b_dim, tiles_per_seq, n_cast):
    cast_src, rest = rest[:n_cast], rest[n_cast:]
    o_ref, rest = rest[0], rest[1:]
    cast_dst, (xn_ref, abuf_ref, cbuf_ref, ya_ref, mix_ref) = rest[:n_cast], rest[n_cast:]
    _cast_slabs(cast_src, cast_dst)
    ts = h_ref.shape[0]
    a_halo = _halo(caw_ref.shape[0])
    b_halo = _halo(cbw_ref.shape[0])
    s = pl.program_id(0)

    @pl.when(s == 0)
    def _():
        _hist_store(abuf_ref, 0, jnp.zeros((a_halo, a_dim), F32))
        _hist_store(cbuf_ref, 0, jnp.zeros((b_halo, b_dim), F32))

    xn_ref[...] = _rms_norm(h_ref[...], g_ref[...]).astype(BF16)

    n_conv_blocks = ts // CONV_ROWS
    z = _dot(xn_ref[...], win_ref[...])
    for p in range(a_dim // GLU_COLS):
        zp = z[:, 2 * p * GLU_COLS:2 * (p + 1) * GLU_COLS]
        glu = _gated(zp[:, 0:GLU_COLS], zp[:, GLU_COLS:2 * GLU_COLS])
        _hist_store(abuf_ref, a_halo, glu, first_block=p * GLU_COLS // LANES)
    zb = z[:, 2 * a_dim:2 * a_dim + 2 * b_dim]
    _hist_store(cbuf_ref, b_halo, zb[:, 0:b_dim] * zb[:, b_dim:2 * b_dim])
    b_gate = z[:, 2 * a_dim + 2 * b_dim:2 * a_dim + 3 * b_dim]

    for j in range(a_dim // LANES):
        lanes = slice(j * LANES, (j + 1) * LANES)
        for blk in range(n_conv_blocks):
            rows = slice(blk * CONV_ROWS, (blk + 1) * CONV_ROWS)
            ya_ref[rows, lanes] = (
                _causal_conv(abuf_ref, caw_ref, a_halo, blk * CONV_ROWS, CONV_ROWS, j)
                + cab_ref[:, lanes])

    for blk in range(n_conv_blocks):
        rows = slice(blk * CONV_ROWS, (blk + 1) * CONV_ROWS)
        ya = _layer_norm(ya_ref[rows, :], lng_ref[...], lnb_ref[...])
        mix_ref[rows, 0:a_dim] = _gated(ya, ya).astype(BF16)
        yb = jnp.concatenate(
            [_causal_conv(cbuf_ref, cbw_ref, b_halo, blk * CONV_ROWS, CONV_ROWS, j)
             for j in range(b_dim // LANES)], axis=1)
        mix_ref[rows, a_dim:a_dim + b_dim] = (b_gate[rows, :] * yb).astype(BF16)

    for blk in range(ts // ROW_BLOCK):
        rows = slice(blk * ROW_BLOCK, (blk + 1) * ROW_BLOCK)
        o_ref[rows, :] = h_ref[rows, :] + _dot(mix_ref[rows, :], wout_ref[...])

    sequence_end = (s + 1) % tiles_per_seq == 0
    _hist_carry(abuf_ref, a_halo, ts, sequence_end)
    _hist_carry(cbuf_ref, b_halo, ts, sequence_end)


def _conv_in_columns(w_in, a_dim, b_dim):
    cols = []
    for p in range(a_dim // GLU_COLS):
        cols.append(w_in[:, p * GLU_COLS:(p + 1) * GLU_COLS])
        cols.append(w_in[:, a_dim + p * GLU_COLS:a_dim + (p + 1) * GLU_COLS])
    cols.append(w_in[:, 2 * a_dim + b_dim:2 * a_dim + 3 * b_dim])
    cols.append(w_in[:, 2 * a_dim:2 * a_dim + b_dim])
    return jnp.concatenate(cols, axis=1)


def _conv_mix(h, g, w_in_cols, conv_a_w, conv_a_b, ln_a_g, ln_a_b, conv_b_w, w_out, *, seq_len,
              cast=()):
    t, d = h.shape
    ka, a_dim = conv_a_w.shape
    kb, b_dim = conv_b_w.shape
    a_halo = _halo(ka)
    b_halo = _halo(kb)
    ts = TOKEN_TILE
    assert seq_len % ts == 0 and t % seq_len == 0
    n_steps = t // ts
    weights = (g.reshape(1, d), w_in_cols, conv_a_w, conv_a_b.reshape(1, a_dim),
               ln_a_g.reshape(1, a_dim), ln_a_b.reshape(1, a_dim), conv_b_w, w_out)
    in_specs = [pl.BlockSpec((ts, d), lambda s: (s, 0))]
    in_specs += [_resident(a.shape) for a in weights]
    cast_in, cast_out, cast_shapes = _cast_stream(cast, n_steps)
    outs = pl.pallas_call(
        functools.partial(_conv_mix_body, a_dim=a_dim, b_dim=b_dim, tiles_per_seq=seq_len // ts,
                          n_cast=len(cast)),
        grid=(n_steps,),
        in_specs=in_specs + cast_in,
        out_specs=[pl.BlockSpec((ts, d), lambda s: (s, 0))] + cast_out,
        out_shape=[jax.ShapeDtypeStruct((t, d), F32)] + cast_shapes,
        scratch_shapes=[
            pltpu.VMEM((ts, d), BF16),
            pltpu.VMEM((a_dim // LANES, HIST_ROW_STRIDE * (a_halo + ts), LANES), F32),
            pltpu.VMEM((b_dim // LANES, HIST_ROW_STRIDE * (b_halo + ts), LANES), F32),
            pltpu.VMEM((ts, a_dim), F32),
            pltpu.VMEM((ts, a_dim + b_dim), BF16),
        ],
        compiler_params=pltpu.CompilerParams(
            dimension_semantics=("arbitrary",), vmem_limit_bytes=VMEM_LIMIT_BYTES),
        name="conv_mix",
    )(h, *weights, *_cast_operands(cast))
    return outs[0], tuple(outs[1:])


def _gelu(x):
    return 0.5 * x * (1.0 + lax.erf(x * (0.5 ** 0.5)))


def _spatial_body(h_ref, g_ref, win_ref, bin_ref, lng_ref, lnb_ref, ws_ref, bs_ref, wout_ref,
                  o_ref, xn_ref, u_ref, v_ref, y_ref, *, c_dim):
    tm = h_ref.shape[0]
    groups, chunk, _ = ws_ref.shape
    gdim = c_dim // groups
    n_chunks = tm // chunk

    for n in range(tm // ROW_BLOCK):
        rows = slice(n * ROW_BLOCK, (n + 1) * ROW_BLOCK)
        xn_ref[rows, :] = _rms_norm(h_ref[rows, :], g_ref[...]).astype(BF16)
        v = _gelu(_dot(xn_ref[rows, :], win_ref[:, c_dim:2 * c_dim]) + bin_ref[:, c_dim:2 * c_dim])
        v_ref[rows, :] = _layer_norm(v, lng_ref[...], lnb_ref[...]).astype(BF16)
    for n in range(tm // ROW_BLOCK):
        rows = slice(n * ROW_BLOCK, (n + 1) * ROW_BLOCK)
        u_ref[rows, :] = _gelu(_dot(xn_ref[rows, :], win_ref[:, 0:c_dim]) + bin_ref[:, 0:c_dim])

    row = lax.broadcasted_iota(jnp.int32, (chunk, chunk), 0)
    col = lax.broadcasted_iota(jnp.int32, (chunk, chunk), 1)
    causal = col <= row
    sv = []
    for gi in range(groups):
        lanes = slice(gi * gdim, (gi + 1) * gdim)
        ws = jnp.where(causal, ws_ref[gi], 0.0).astype(BF16)
        vg = jnp.concatenate(
            [v_ref[n * chunk:(n + 1) * chunk, lanes] for n in range(n_chunks)], axis=1)
        sv.append(_dot(ws, vg))
    chunks_per_block = ROW_BLOCK // chunk
    for blk in range(tm // ROW_BLOCK):
        y = []
        for n in range(blk * chunks_per_block, (blk + 1) * chunks_per_block):
            gate = jnp.concatenate(
                [sv[gi][:, n * gdim:(n + 1) * gdim] + bs_ref[gi] for gi in range(groups)], axis=1)
            y.append((u_ref[n * chunk:(n + 1) * chunk, :] * gate).astype(BF16))
        rows = slice(blk * ROW_BLOCK, (blk + 1) * ROW_BLOCK)
        o_ref[rows, :] = h_ref[rows, :] + _dot(jnp.concatenate(y, axis=0), wout_ref[...])


def _spatial_gating(h, g, w_in, b_in, ln_v_g, ln_v_b, w_s, b_s, w_out, *, seq_len):
    t, d = h.shape
    c_dim = w_out.shape[0]
    groups, chunk, _ = w_s.shape
    gdim = c_dim // groups
    assert TOKEN_TILE % chunk == 0 and seq_len % TOKEN_TILE == 0
    bs = jnp.broadcast_to(b_s[:, :, None], (groups, chunk, gdim))
    args = (h, g.reshape(1, d), w_in, b_in.reshape(1, 2 * c_dim),
            ln_v_g.reshape(1, c_dim), ln_v_b.reshape(1, c_dim), w_s, bs, w_out)
    in_specs = [pl.BlockSpec((TOKEN_TILE, d), lambda i: (i, 0))]
    in_specs += [_resident(a.shape) for a in args[1:]]
    return pl.pallas_call(
        functools.partial(_spatial_body, c_dim=c_dim),
        grid=(t // TOKEN_TILE,),
        in_specs=in_specs,
        out_specs=pl.BlockSpec((TOKEN_TILE, d), lambda i: (i, 0)),
        out_shape=jax.ShapeDtypeStruct((t, d), F32),
        scratch_shapes=[
            pltpu.VMEM((TOKEN_TILE, d), BF16),
            pltpu.VMEM((TOKEN_TILE, c_dim), F32),
            pltpu.VMEM((TOKEN_TILE, c_dim), BF16),
            pltpu.VMEM((TOKEN_TILE, c_dim), BF16),
        ],
        compiler_params=pltpu.CompilerParams(
            dimension_semantics=("arbitrary",), vmem_limit_bytes=VMEM_LIMIT_BYTES),
        name="spatial_gating",
    )(*args)


def kernel(x, ev_norm_g, ev_w_in, ev_conv_a_w, ev_conv_a_b, ev_ln_a_g, ev_ln_a_b, ev_conv_b_w, ev_w_out, od_norm_g, od_w_in, od_b_in, od_ln_v_g, od_ln_v_b, od_w_s, od_b_s, od_w_out, mlp_norm_g, mlp_w1, mlp_w2, final_norm_g):
    bsz, s, d = x.shape
    depth = mlp_w1.shape[0]
    a_dim = ev_conv_a_w.shape[2]
    b_dim = ev_conv_b_w.shape[2]

    def layer(stack, k, stream):
        return _Layer(stack, k) if stream else stack[k]

    def mixer_weights(i, stream):
        j = i // 2
        if i % 2 == 0:
            return (_conv_in_columns(ev_w_in[j], a_dim, b_dim), layer(ev_w_out, j, stream))
        return (layer(od_w_in, j, stream), layer(od_w_out, j, stream))

    def to_bf16(ws):
        return tuple(w.astype(BF16) for w in ws)

    mix_w = to_bf16(mixer_weights(0, stream=False))
    mlp_w = None
    h = x.reshape(bsz * s, d)
    for i in range(depth):
        j = i // 2
        if i % 2 == 0:
            h, mlp_w = _conv_mix(h, ev_norm_g[j], mix_w[0], ev_conv_a_w[j], ev_conv_a_b[j],
                                 ev_ln_a_g[j], ev_ln_a_b[j], ev_conv_b_w[j], mix_w[1],
                                 seq_len=s, cast=(_Layer(mlp_w1, i), _Layer(mlp_w2, i)))
        else:
            h = _spatial_gating(h, od_norm_g[j], mix_w[0], od_b_in[j], od_ln_v_g[j], od_ln_v_b[j],
                                od_w_s[j], od_b_s[j], mix_w[1], seq_len=s)
            if mlp_w is None:
                mlp_w = to_bf16((mlp_w1[i], mlp_w2[i]))
        last = i == depth - 1
        cast = ()
        if not last:
            cast = mixer_weights(i + 1, stream=True)
            if (i + 1) % 2 == 1:
                cast = cast + (_Layer(mlp_w1, i + 1), _Layer(mlp_w2, i + 1))
        h, staged = _mlp(h, mlp_norm_g[i], mlp_w[0], mlp_w[1], final_norm_g if last else None,
                         cast=cast)
        mix_w, mlp_w = staged[0:2], (staged[2:4] or None)
    return h.reshape(bsz, s, d)
```

```python
import functools
from typing import NamedTuple

import jax
import jax.numpy as jnp
from jax import lax
from jax.experimental import pallas as pl
from jax.experimental.pallas import tpu as pltpu

F32 = jnp.float32
BF16 = jnp.bfloat16

RMS_EPS = 1e-6
LN_EPS = 1e-5

SUBLANES = 8
LANES = 128
BF16_SUBLANES = 16
VMEM_LIMIT_BYTES = 56 * 1024 * 1024

TOKEN_TILE = 1024
MLP_TOKEN_TILE = 1024
FF_CHUNK = 512
ROW_BLOCK = 256
CONV_ROWS = 32
GLU_COLS = 256
HIST_ROW_STRIDE = 2


def _rms_norm(x, g):
    ms = jnp.mean(x * x, axis=-1, keepdims=True)
    return x * lax.rsqrt(ms + RMS_EPS) * g


def _layer_norm(x, g, b):
    mu = jnp.mean(x, axis=-1, keepdims=True)
    xc = x - mu
    var = jnp.mean(xc * xc, axis=-1, keepdims=True)
    return xc * lax.rsqrt(var + LN_EPS) * g + b


def _zero_after(v):
    return jnp.minimum(jnp.abs(v), 0.0)


def _dot(a, b):
    return jnp.dot(a, b, preferred_element_type=F32)


def _resident(shape):
    zeros = (0,) * len(shape)
    return pl.BlockSpec(shape, lambda *_: zeros, pipeline_mode=pl.Buffered(1))


def _halo(width):
    return -(-(width - 1) // SUBLANES) * SUBLANES


class _Layer(NamedTuple):
    stack: jax.Array
    index: int

    @property
    def shape(self):
        return self.stack.shape


def _cast_operands(weights):
    return [w.stack if isinstance(w, _Layer) else w for w in weights]


def _cast_stream(weights, n_steps):
    in_specs, out_specs, out_shapes = [], [], []
    for w in weights:
        rows, cols = w.shape[-2:]
        slab = rows // n_steps
        assert slab * n_steps == rows and slab % BF16_SUBLANES == 0
        if isinstance(w, _Layer):
            in_specs.append(pl.BlockSpec((None, slab, cols), lambda i, k=w.index: (k, i, 0)))
        else:
            in_specs.append(pl.BlockSpec((slab, cols), lambda i: (i, 0)))
        out_specs.append(pl.BlockSpec((slab, cols), lambda i: (i, 0)))
        out_shapes.append(jax.ShapeDtypeStruct((rows, cols), BF16))
    return in_specs, out_specs, out_shapes


def _cast_slabs(src_refs, dst_refs):
    for src_ref, dst_ref in zip(src_refs, dst_refs):
        dst_ref[...] = src_ref[...].astype(BF16)


def _mlp_body(h_ref, g_ref, w1_ref, w2_ref, *rest, final_norm, n_cast):
    if final_norm:
        gf_ref, rest = rest[0], rest[1:]
    cast_src, rest = rest[:n_cast], rest[n_cast:]
    o_ref, rest = rest[0], rest[1:]
    cast_dst, (xn_ref, acc_ref) = rest[:n_cast], rest[n_cast:]
    _cast_slabs(cast_src, cast_dst)
    xn_ref[...] = _rms_norm(h_ref[...], g_ref[...]).astype(BF16)
    fc = FF_CHUNK
    for c in range(w1_ref.shape[1] // fc):
        a = jnp.maximum(_dot(xn_ref[...], w1_ref[:, c * fc:(c + 1) * fc]), 0.0)
        upd = _dot((a * a).astype(BF16), w2_ref[c * fc:(c + 1) * fc, :])
        if c == 0:
            acc_ref[...] = upd
        else:
            acc_ref[...] += upd
    y = h_ref[...] + acc_ref[...]
    if final_norm:
        y = _rms_norm(y, gf_ref[...])
    o_ref[...] = y


def _mlp(h, g, w1, w2, final_g=None, cast=()):
    t, d = h.shape
    final_norm = final_g is not None
    n_steps = t // MLP_TOKEN_TILE
    args = [h, g.reshape(1, d), w1, w2]
    in_specs = [
        pl.BlockSpec((MLP_TOKEN_TILE, d), lambda i: (i, 0)),
        _resident((1, d)),
        _resident(w1.shape),
        _resident(w2.shape),
    ]
    if final_norm:
        args.append(final_g.reshape(1, d))
        in_specs.append(_resident((1, d)))
    cast_in, cast_out, cast_shapes = _cast_stream(cast, n_steps)
    outs = pl.pallas_call(
        functools.partial(_mlp_body, final_norm=final_norm, n_cast=len(cast)),
        grid=(n_steps,),
        in_specs=in_specs + cast_in,
        out_specs=[pl.BlockSpec((MLP_TOKEN_TILE, d), lambda i: (i, 0))] + cast_out,
        out_shape=[jax.ShapeDtypeStruct((t, d), F32)] + cast_shapes,
        scratch_shapes=[
            pltpu.VMEM((MLP_TOKEN_TILE, d), BF16),
            pltpu.VMEM((MLP_TOKEN_TILE, d), F32),
        ],
        compiler_params=pltpu.CompilerParams(
            dimension_semantics=("arbitrary",), vmem_limit_bytes=VMEM_LIMIT_BYTES),
        name="mlp_final" if final_norm else "mlp",
    )(*args, *_cast_operands(cast))
    return outs[0], tuple(outs[1:])


def _hist_rows(start, size):
    return pl.ds(HIST_ROW_STRIDE * start, size, stride=HIST_ROW_STRIDE)


def _hist_store(buf_ref, start, x, first_block=0):
    for j in range(x.shape[1] // LANES):
        buf_ref[first_block + j, _hist_rows(start, x.shape[0]), :] = x[:, j * LANES:(j + 1) * LANES]


def _hist_carry(buf_ref, halo, ts, sequence_end):
    for j in range(buf_ref.shape[0]):
        tail = buf_ref[j, _hist_rows(ts, halo), :]
        buf_ref[j, _hist_rows(0, halo), :] = jnp.where(sequence_end, 0.0, tail)


def _causal_conv(buf_ref, w_ref, halo, row0, rows, j):
    taps = w_ref.shape[0]
    acc = None
    for k in range(taps):
        start = halo + row0 - (taps - 1 - k)
        term = buf_ref[j, _hist_rows(start, rows), :] * w_ref[k:k + 1, j * LANES:(j + 1) * LANES]
        acc = term if acc is None else acc + term
    return acc


def _conv_mix_body(h_ref, g_ref, win_ref, caw_ref, cab_ref, lng_ref, lnb_ref, cbw_ref, wout_ref,
                   *rest, a_dim, b_dim, tiles_per_seq, n_cast):
    cast_src, rest = rest[:n_cast], rest[n_cast:]
    o_ref, rest = rest[0], rest[1:]
    cast_dst, (xn_ref, abuf_ref, cbuf_ref, ya_ref, mix_ref) = rest[:n_cast], rest[n_cast:]
    _cast_slabs(cast_src, cast_dst)
    ts = h_ref.shape[0]
    a_halo = _halo(caw_ref.shape[0])
    b_halo = _halo(cbw_ref.shape[0])
    s = pl.program_id(0)

    @pl.when(s == 0)
    def _():
        _hist_store(abuf_ref, 0, jnp.zeros((a_halo, a_dim), F32))
        _hist_store(cbuf_ref, 0, jnp.zeros((b_halo, b_dim), F32))

    xn_ref[...] = _rms_norm(h_ref[...], g_ref[...]).astype(BF16)

    n_conv_blocks = ts // CONV_ROWS
    z = _dot(xn_ref[...], win_ref[...])
    for p in range(a_dim // GLU_COLS):
        zp = z[:, 2 * p * GLU_COLS:2 * (p + 1) * GLU_COLS]
        glu = zp[:, 0:GLU_COLS] * jax.nn.sigmoid(zp[:, GLU_COLS:2 * GLU_COLS])
        _hist_store(abuf_ref, a_halo, glu, first_block=p * GLU_COLS // LANES)
    zb = z[:, 2 * a_dim:2 * a_dim + 2 * b_dim]
    _hist_store(cbuf_ref, b_halo, zb[:, 0:b_dim] * zb[:, b_dim:2 * b_dim])
    b_gate = z[:, 2 * a_dim + 2 * b_dim:2 * a_dim + 3 * b_dim]

    for j in range(a_dim // LANES):
        lanes = slice(j * LANES, (j + 1) * LANES)
        for blk in range(n_conv_blocks):
            rows = slice(blk * CONV_ROWS, (blk + 1) * CONV_ROWS)
            ya_ref[rows, lanes] = (
                _causal_conv(abuf_ref, caw_ref, a_halo, blk * CONV_ROWS, CONV_ROWS, j)
                + cab_ref[:, lanes])

    for blk in range(n_conv_blocks):
        rows = slice(blk * CONV_ROWS, (blk + 1) * CONV_ROWS)
        anchor = (_zero_after(b_gate[rows, b_dim - LANES:b_dim])
                  + _zero_after(zb[rows, 2 * b_dim - LANES:2 * b_dim]))
        ya = ya_ref[rows, :] + jnp.tile(anchor, (1, a_dim // LANES))
        ya = _layer_norm(ya, lng_ref[...], lnb_ref[...])
        mix_ref[rows, 0:a_dim] = (ya * jax.nn.sigmoid(ya)).astype(BF16)
        yb = jnp.concatenate(
            [_causal_conv(cbuf_ref, cbw_ref, b_halo, blk * CONV_ROWS, CONV_ROWS, j)
             for j in range(b_dim // LANES)], axis=1)
        mix_ref[rows, a_dim:a_dim + b_dim] = (b_gate[rows, :] * yb).astype(BF16)

    for blk in range(ts // ROW_BLOCK):
        rows = slice(blk * ROW_BLOCK, (blk + 1) * ROW_BLOCK)
        o_ref[rows, :] = h_ref[rows, :] + _dot(mix_ref[rows, :], wout_ref[...])

    sequence_end = (s + 1) % tiles_per_seq == 0
    _hist_carry(abuf_ref, a_halo, ts, sequence_end)
    _hist_carry(cbuf_ref, b_halo, ts, sequence_end)


def _conv_in_columns(w_in, a_dim, b_dim):
    cols = []
    for p in range(a_dim // GLU_COLS):
        cols.append(w_in[:, p * GLU_COLS:(p + 1) * GLU_COLS])
        cols.append(w_in[:, a_dim + p * GLU_COLS:a_dim + (p + 1) * GLU_COLS])
    cols.append(w_in[:, 2 * a_dim + b_dim:2 * a_dim + 3 * b_dim])
    cols.append(w_in[:, 2 * a_dim:2 * a_dim + b_dim])
    return jnp.concatenate(cols, axis=1)


def _conv_mix(h, g, w_in_cols, conv_a_w, conv_a_b, ln_a_g, ln_a_b, conv_b_w, w_out, *, seq_len,
              cast=()):
    t, d = h.shape
    ka, a_dim = conv_a_w.shape
    kb, b_dim = conv_b_w.shape
    a_halo = _halo(ka)
    b_halo = _halo(kb)
    ts = TOKEN_TILE
    assert seq_len % ts == 0 and t % seq_len == 0
    n_steps = t // ts
    weights = (g.reshape(1, d), w_in_cols, conv_a_w, conv_a_b.reshape(1, a_dim),
               ln_a_g.reshape(1, a_dim), ln_a_b.reshape(1, a_dim), conv_b_w, w_out)
    in_specs = [pl.BlockSpec((ts, d), lambda s: (s, 0))]
    in_specs += [_resident(a.shape) for a in weights]
    cast_in, cast_out, cast_shapes = _cast_stream(cast, n_steps)
    outs = pl.pallas_call(
        functools.partial(_conv_mix_body, a_dim=a_dim, b_dim=b_dim, tiles_per_seq=seq_len // ts,
                          n_cast=len(cast)),
        grid=(n_steps,),
        in_specs=in_specs + cast_in,
        out_specs=[pl.BlockSpec((ts, d), lambda s: (s, 0))] + cast_out,
        out_shape=[jax.ShapeDtypeStruct((t, d), F32)] + cast_shapes,
        scratch_shapes=[
            pltpu.VMEM((ts, d), BF16),
            pltpu.VMEM((a_dim // LANES, HIST_ROW_STRIDE * (a_halo + ts), LANES), F32),
            pltpu.VMEM((b_dim // LANES, HIST_ROW_STRIDE * (b_halo + ts), LANES), F32),
            pltpu.VMEM((ts, a_dim), F32),
            pltpu.VMEM((ts, a_dim + b_dim), BF16),
        ],
        compiler_params=pltpu.CompilerParams(
            dimension_semantics=("arbitrary",), vmem_limit_bytes=VMEM_LIMIT_BYTES),
        name="conv_mix",
    )(h, *weights, *_cast_operands(cast))
    return outs[0], tuple(outs[1:])


def _gelu(x):
    return 0.5 * x * (1.0 + lax.erf(x * (0.5 ** 0.5)))


def _spatial_body(h_ref, g_ref, win_ref, bin_ref, lng_ref, lnb_ref, ws_ref, bs_ref, wout_ref,
                  o_ref, xn_ref, u_ref, v_ref, y_ref, *, c_dim):
    tm = h_ref.shape[0]
    groups, chunk, _ = ws_ref.shape
    gdim = c_dim // groups
    n_chunks = tm // chunk

    for n in range(tm // ROW_BLOCK):
        rows = slice(n * ROW_BLOCK, (n + 1) * ROW_BLOCK)
        xn_ref[rows, :] = _rms_norm(h_ref[rows, :], g_ref[...]).astype(BF16)
        v = _gelu(_dot(xn_ref[rows, :], win_ref[:, c_dim:2 * c_dim]) + bin_ref[:, c_dim:2 * c_dim])
        v_ref[rows, :] = _layer_norm(v, lng_ref[...], lnb_ref[...]).astype(BF16)
    for n in range(tm // ROW_BLOCK):
        rows = slice(n * ROW_BLOCK, (n + 1) * ROW_BLOCK)
        u_ref[rows, :] = _gelu(_dot(xn_ref[rows, :], win_ref[:, 0:c_dim]) + bin_ref[:, 0:c_dim])

    row = lax.broadcasted_iota(jnp.int32, (chunk, chunk), 0)
    col = lax.broadcasted_iota(jnp.int32, (chunk, chunk), 1)
    causal = col <= row
    sv = []
    for gi in range(groups):
        lanes = slice(gi * gdim, (gi + 1) * gdim)
        ws = jnp.where(causal, ws_ref[gi], 0.0).astype(BF16)
        vg = jnp.concatenate(
            [v_ref[n * chunk:(n + 1) * chunk, lanes] for n in range(n_chunks)], axis=1)
        sv.append(_dot(ws, vg))
    chunks_per_block = ROW_BLOCK // chunk
    for blk in range(tm // ROW_BLOCK):
        y = []
        for n in range(blk * chunks_per_block, (blk + 1) * chunks_per_block):
            gate = jnp.concatenate(
                [sv[gi][:, n * gdim:(n + 1) * gdim] + bs_ref[gi] for gi in range(groups)], axis=1)
            y.append((u_ref[n * chunk:(n + 1) * chunk, :] * gate).astype(BF16))
        rows = slice(blk * ROW_BLOCK, (blk + 1) * ROW_BLOCK)
        o_ref[rows, :] = h_ref[rows, :] + _dot(jnp.concatenate(y, axis=0), wout_ref[...])


def _spatial_gating(h, g, w_in, b_in, ln_v_g, ln_v_b, w_s, b_s, w_out, *, seq_len):
    t, d = h.shape
    c_dim = w_out.shape[0]
    groups, chunk, _ = w_s.shape
    gdim = c_dim // groups
    assert TOKEN_TILE % chunk == 0 and seq_len % TOKEN_TILE == 0
    bs = jnp.broadcast_to(b_s[:, :, None], (groups, chunk, gdim))
    args = (h, g.reshape(1, d), w_in, b_in.reshape(1, 2 * c_dim),
            ln_v_g.reshape(1, c_dim), ln_v_b.reshape(1, c_dim), w_s, bs, w_out)
    in_specs = [pl.BlockSpec((TOKEN_TILE, d), lambda i: (i, 0))]
    in_specs += [_resident(a.shape) for a in args[1:]]
    return pl.pallas_call(
        functools.partial(_spatial_body, c_dim=c_dim),
        grid=(t // TOKEN_TILE,),
        in_specs=in_specs,
        out_specs=pl.BlockSpec((TOKEN_TILE, d), lambda i: (i, 0)),
        out_shape=jax.ShapeDtypeStruct((t, d), F32),
        scratch_shapes=[
            pltpu.VMEM((TOKEN_TILE, d), BF16),
            pltpu.VMEM((TOKEN_TILE, c_dim), F32),
            pltpu.VMEM((TOKEN_TILE, c_dim), BF16),
            pltpu.VMEM((TOKEN_TILE, c_dim), BF16),
        ],
        compiler_params=pltpu.CompilerParams(
            dimension_semantics=("arbitrary",), vmem_limit_bytes=VMEM_LIMIT_BYTES),
        name="spatial_gating",
    )(*args)


def kernel(x, ev_norm_g, ev_w_in, ev_conv_a_w, ev_conv_a_b, ev_ln_a_g, ev_ln_a_b, ev_conv_b_w, ev_w_out, od_norm_g, od_w_in, od_b_in, od_ln_v_g, od_ln_v_b, od_w_s, od_b_s, od_w_out, mlp_norm_g, mlp_w1, mlp_w2, final_norm_g):
    bsz, s, d = x.shape
    depth = mlp_w1.shape[0]
    a_dim = ev_conv_a_w.shape[2]
    b_dim = ev_conv_b_w.shape[2]

    def layer(stack, k, stream):
        return _Layer(stack, k) if stream else stack[k]

    def mixer_weights(i, stream):
        j = i // 2
        if i % 2 == 0:
            return (_conv_in_columns(ev_w_in[j], a_dim, b_dim), layer(ev_w_out, j, stream))
        return (layer(od_w_in, j, stream), layer(od_w_out, j, stream))

    def to_bf16(ws):
        return tuple(w.astype(BF16) for w in ws)

    mix_w = to_bf16(mixer_weights(0, stream=False))
    mlp_w = None
    h = x.reshape(bsz * s, d)
    for i in range(depth):
        j = i // 2
        if i % 2 == 0:
            h, mlp_w = _conv_mix(h, ev_norm_g[j], mix_w[0], ev_conv_a_w[j], ev_conv_a_b[j],
                                 ev_ln_a_g[j], ev_ln_a_b[j], ev_conv_b_w[j], mix_w[1],
                                 seq_len=s, cast=(_Layer(mlp_w1, i), _Layer(mlp_w2, i)))
        else:
            h = _spatial_gating(h, od_norm_g[j], mix_w[0], od_b_in[j], od_ln_v_g[j], od_ln_v_b[j],
                                od_w_s[j], od_b_s[j], mix_w[1], seq_len=s)
            if mlp_w is None:
                mlp_w = to_bf16((mlp_w1[i], mlp_w2[i]))
        last = i == depth - 1
        cast = ()
        if not last:
            cast = mixer_weights(i + 1, stream=True)
            if (i + 1) % 2 == 1:
                cast = cast + (_Layer(mlp_w1, i + 1), _Layer(mlp_w2, i + 1))
        h, staged = _mlp(h, mlp_norm_g[i], mlp_w[0], mlp_w[1], final_norm_g if last else None,
                         cast=cast)
        mix_w, mlp_w = staged[0:2], (staged[2:4] or None)
    return h.reshape(bsz, s, d)
```

```python
import functools
from typing import NamedTuple

import jax
import jax.numpy as jnp
from jax import lax
from jax.experimental import pallas as pl
from jax.experimental.pallas import tpu as pltpu

F32 = jnp.float32
BF16 = jnp.bfloat16

RMS_EPS = 1e-6
LN_EPS = 1e-5

SUBLANES = 8
LANES = 128
BF16_SUBLANES = 16
VMEM_LIMIT_BYTES = 56 * 1024 * 1024

TOKEN_TILE = 1024
MLP_TOKEN_TILE = 1024
FF_CHUNK = 512
ROW_BLOCK = 256
CONV_ROWS = 32
GLU_COLS = 256
HIST_ROW_STRIDE = 2


def _rms_norm(x, g):
    ms = jnp.mean(x * x, axis=-1, keepdims=True)
    return x * lax.rsqrt(ms + RMS_EPS) * g


def _layer_norm(x, g, b):
    mu = jnp.mean(x, axis=-1, keepdims=True)
    xc = x - mu
    var = jnp.mean(xc * xc, axis=-1, keepdims=True)
    return xc * lax.rsqrt(var + LN_EPS) * g + b


def _zero_after(v):
    return jnp.minimum(jnp.abs(v), 0.0)


def _dot(a, b):
    return jnp.dot(a, b, preferred_element_type=F32)


def _resident(shape):
    zeros = (0,) * len(shape)
    return pl.BlockSpec(shape, lambda *_: zeros, pipeline_mode=pl.Buffered(1))


def _halo(width):
    return -(-(width - 1) // SUBLANES) * SUBLANES


class _Layer(NamedTuple):
    stack: jax.Array
    index: int

    @property
    def shape(self):
        return self.stack.shape


def _cast_operands(weights):
    return [w.stack if isinstance(w, _Layer) else w for w in weights]


def _cast_stream(weights, n_steps):
    in_specs, out_specs, out_shapes = [], [], []
    for w in weights:
        rows, cols = w.shape[-2:]
        slab = rows // n_steps
        assert slab * n_steps == rows and slab % BF16_SUBLANES == 0
        if isinstance(w, _Layer):
            in_specs.append(pl.BlockSpec((None, slab, cols), lambda i, k=w.index: (k, i, 0)))
        else:
            in_specs.append(pl.BlockSpec((slab, cols), lambda i: (i, 0)))
        out_specs.append(pl.BlockSpec((slab, cols), lambda i: (i, 0)))
        out_shapes.append(jax.ShapeDtypeStruct((rows, cols), BF16))
    return in_specs, out_specs, out_shapes


def _cast_slabs(src_refs, dst_refs):
    for src_ref, dst_ref in zip(src_refs, dst_refs):
        dst_ref[...] = src_ref[...].astype(BF16)


def _mlp_body(h_ref, g_ref, w1_ref, w2_ref, *rest, final_norm, n_cast):
    if final_norm:
        gf_ref, rest = rest[0], rest[1:]
    cast_src, rest = rest[:n_cast], rest[n_cast:]
    o_ref, rest = rest[0], rest[1:]
    cast_dst, (xn_ref, acc_ref) = rest[:n_cast], rest[n_cast:]
    _cast_slabs(cast_src, cast_dst)
    xn_ref[...] = _rms_norm(h_ref[...], g_ref[...]).astype(BF16)
    fc = FF_CHUNK
    n_chunks = w1_ref.shape[1] // fc

    def hidden(c):
        a = jnp.maximum(_dot(xn_ref[...], w1_ref[:, c * fc:(c + 1) * fc]), 0.0)
        return (a * a).astype(BF16)

    acc_ref[...] = _dot(hidden(0), w2_ref[0:fc, :])
    for c in range(1, n_chunks - 1):
        acc_ref[...] += _dot(hidden(c), w2_ref[c * fc:(c + 1) * fc, :])
    last = hidden(n_chunks - 1)
    for r0 in range(0, h_ref.shape[0], ROW_BLOCK):
        rows = slice(r0, r0 + ROW_BLOCK)
        y = h_ref[rows, :] + (acc_ref[rows, :] + _dot(last[rows, :], w2_ref[(n_chunks - 1) * fc:, :]))
        if final_norm:
            y = _rms_norm(y, gf_ref[...])
        o_ref[rows, :] = y


def _mlp(h, g, w1, w2, final_g=None, cast=()):
    t, d = h.shape
    final_norm = final_g is not None
    n_steps = t // MLP_TOKEN_TILE
    args = [h, g.reshape(1, d), w1, w2]
    in_specs = [
        pl.BlockSpec((MLP_TOKEN_TILE, d), lambda i: (i, 0)),
        _resident((1, d)),
        _resident(w1.shape),
        _resident(w2.shape),
    ]
    if final_norm:
        args.append(final_g.reshape(1, d))
        in_specs.append(_resident((1, d)))
    cast_in, cast_out, cast_shapes = _cast_stream(cast, n_steps)
    outs = pl.pallas_call(
        functools.partial(_mlp_body, final_norm=final_norm, n_cast=len(cast)),
        grid=(n_steps,),
        in_specs=in_specs + cast_in,
        out_specs=[pl.BlockSpec((MLP_TOKEN_TILE, d), lambda i: (i, 0))] + cast_out,
        out_shape=[jax.ShapeDtypeStruct((t, d), F32)] + cast_shapes,
        scratch_shapes=[
            pltpu.VMEM((MLP_TOKEN_TILE, d), BF16),
            pltpu.VMEM((MLP_TOKEN_TILE, d), F32),
        ],
        compiler_params=pltpu.CompilerParams(
            dimension_semantics=("arbitrary",), vmem_limit_bytes=VMEM_LIMIT_BYTES),
        name="mlp_final" if final_norm else "mlp",
    )(*args, *_cast_operands(cast))
    return outs[0], tuple(outs[1:])


def _hist_rows(start, size):
    return pl.ds(HIST_ROW_STRIDE * start, size, stride=HIST_ROW_STRIDE)


def _hist_store(buf_ref, start, x, first_block=0):
    for j in range(x.shape[1] // LANES):
        buf_ref[first_block + j, _hist_rows(start, x.shape[0]), :] = x[:, j * LANES:(j + 1) * LANES]


def _hist_carry(buf_ref, halo, ts, sequence_end):
    for j in range(buf_ref.shape[0]):
        tail = buf_ref[j, _hist_rows(ts, halo), :]
        buf_ref[j, _hist_rows(0, halo), :] = jnp.where(sequence_end, 0.0, tail)


def _causal_conv(buf_ref, w_ref, halo, row0, rows, j):
    taps = w_ref.shape[0]
    acc = None
    for k in range(taps):
        start = halo + row0 - (taps - 1 - k)
        term = buf_ref[j, _hist_rows(start, rows), :] * w_ref[k:k + 1, j * LANES:(j + 1) * LANES]
        acc = term if acc is None else acc + term
    return acc


def _conv_mix_body(h_ref, g_ref, win_ref, caw_ref, cab_ref, lng_ref, lnb_ref, cbw_ref, wout_ref,
                   *rest, a_dim, b_dim, tiles_per_seq, n_cast):
    cast_src, rest = rest[:n_cast], rest[n_cast:]
    o_ref, rest = rest[0], rest[1:]
    cast_dst, (xn_ref, abuf_ref, cbuf_ref, ya_ref, mix_ref) = rest[:n_cast], rest[n_cast:]
    _cast_slabs(cast_src, cast_dst)
    ts = h_ref.shape[0]
    a_halo = _halo(caw_ref.shape[0])
    b_halo = _halo(cbw_ref.shape[0])
    s = pl.program_id(0)

    @pl.when(s == 0)
    def _():
        _hist_store(abuf_ref, 0, jnp.zeros((a_halo, a_dim), F32))
        _hist_store(cbuf_ref, 0, jnp.zeros((b_halo, b_dim), F32))

    xn_ref[...] = _rms_norm(h_ref[...], g_ref[...]).astype(BF16)

    n_conv_blocks = ts // CONV_ROWS
    z = _dot(xn_ref[...], win_ref[...])
    for p in range(a_dim // GLU_COLS):
        zp = z[:, 2 * p * GLU_COLS:2 * (p + 1) * GLU_COLS]
        glu = zp[:, 0:GLU_COLS] * jax.nn.sigmoid(zp[:, GLU_COLS:2 * GLU_COLS])
        _hist_store(abuf_ref, a_halo, glu, first_block=p * GLU_COLS // LANES)
    zb = z[:, 2 * a_dim:2 * a_dim + 2 * b_dim]
    _hist_store(cbuf_ref, b_halo, zb[:, 0:b_dim] * zb[:, b_dim:2 * b_dim])
    b_gate = z[:, 2 * a_dim + 2 * b_dim:2 * a_dim + 3 * b_dim]

    for j in range(a_dim // LANES):
        lanes = slice(j * LANES, (j + 1) * LANES)
        for blk in range(n_conv_blocks):
            rows = slice(blk * CONV_ROWS, (blk + 1) * CONV_ROWS)
            ya_ref[rows, lanes] = (
                _causal_conv(abuf_ref, caw_ref, a_halo, blk * CONV_ROWS, CONV_ROWS, j)
                + cab_ref[:, lanes])

    for blk in range(n_conv_blocks):
        rows = slice(blk * CONV_ROWS, (blk + 1) * CONV_ROWS)
        anchor = (_zero_after(b_gate[rows, b_dim - LANES:b_dim])
                  + _zero_after(zb[rows, 2 * b_dim - LANES:2 * b_dim]))
        ya = ya_ref[rows, :] + jnp.tile(anchor, (1, a_dim // LANES))
        ya = _layer_norm(ya, lng_ref[...], lnb_ref[...])
        mix_ref[rows, 0:a_dim] = (ya * jax.nn.sigmoid(ya)).astype(BF16)
        yb = jnp.concatenate(
            [_causal_conv(cbuf_ref, cbw_ref, b_halo, blk * CONV_ROWS, CONV_ROWS, j)
             for j in range(b_dim // LANES)], axis=1)
        mix_ref[rows, a_dim:a_dim + b_dim] = (b_gate[rows, :] * yb).astype(BF16)

    for blk in range(ts // ROW_BLOCK):
        rows = slice(blk * ROW_BLOCK, (blk + 1) * ROW_BLOCK)
        o_ref[rows, :] = h_ref[rows, :] + _dot(mix_ref[rows, :], wout_ref[...])

    sequence_end = (s + 1) % tiles_per_seq == 0
    _hist_carry(abuf_ref, a_halo, ts, sequence_end)
    _hist_carry(cbuf_ref, b_halo, ts, sequence_end)


def _conv_in_columns(w_in, a_dim, b_dim):
    cols = []
    for p in range(a_dim // GLU_COLS):
        cols.append(w_in[:, p * GLU_COLS:(p + 1) * GLU_COLS])
        cols.append(w_in[:, a_dim + p * GLU_COLS:a_dim + (p + 1) * GLU_COLS])
    cols.append(w_in[:, 2 * a_dim + b_dim:2 * a_dim + 3 * b_dim])
    cols.append(w_in[:, 2 * a_dim:2 * a_dim + b_dim])
    return jnp.concatenate(cols, axis=1)


def _conv_mix(h, g, w_in_cols, conv_a_w, conv_a_b, ln_a_g, ln_a_b, conv_b_w, w_out, *, seq_len,
              cast=()):
    t, d = h.shape
    ka, a_dim = conv_a_w.shape
    kb, b_dim = conv_b_w.shape
    a_halo = _halo(ka)
    b_halo = _halo(kb)
    ts = TOKEN_TILE
    assert seq_len % ts == 0 and t % seq_len == 0
    n_steps = t // ts
    weights = (g.reshape(1, d), w_in_cols, conv_a_w, conv_a_b.reshape(1, a_dim),
               ln_a_g.reshape(1, a_dim), ln_a_b.reshape(1, a_dim), conv_b_w, w_out)
    in_specs = [pl.BlockSpec((ts, d), lambda s: (s, 0))]
    in_specs += [_resident(a.shape) for a in weights]
    cast_in, cast_out, cast_shapes = _cast_stream(cast, n_steps)
    outs = pl.pallas_call(
        functools.partial(_conv_mix_body, a_dim=a_dim, b_dim=b_dim, tiles_per_seq=seq_len // ts,
                          n_cast=len(cast)),
        grid=(n_steps,),
        in_specs=in_specs + cast_in,
        out_specs=[pl.BlockSpec((ts, d), lambda s: (s, 0))] + cast_out,
        out_shape=[jax.ShapeDtypeStruct((t, d), F32)] + cast_shapes,
        scratch_shapes=[
            pltpu.VMEM((ts, d), BF16),
            pltpu.VMEM((a_dim // LANES, HIST_ROW_STRIDE * (a_halo + ts), LANES), F32),
            pltpu.VMEM((b_dim // LANES, HIST_ROW_STRIDE * (b_halo + ts), LANES), F32),
            pltpu.VMEM((ts, a_dim), F32),
            pltpu.VMEM((ts, a_dim + b_dim), BF16),
        ],
        compiler_params=pltpu.CompilerParams(
            dimension_semantics=("arbitrary",), vmem_limit_bytes=VMEM_LIMIT_BYTES),
        name="conv_mix",
    )(h, *weights, *_cast_operands(cast))
    return outs[0], tuple(outs[1:])


def _gelu(x):
    return 0.5 * x * (1.0 + lax.erf(x * (0.5 ** 0.5)))


def _spatial_body(h_ref, g_ref, win_ref, bin_ref, lng_ref, lnb_ref, ws_ref, bs_ref, wout_ref,
                  o_ref, xn_ref, u_ref, v_ref, y_ref, *, c_dim):
    tm = h_ref.shape[0]
    groups, chunk, _ = ws_ref.shape
    gdim = c_dim // groups
    n_chunks = tm // chunk

    for n in range(tm // ROW_BLOCK):
        rows = slice(n * ROW_BLOCK, (n + 1) * ROW_BLOCK)
        xn_ref[rows, :] = _rms_norm(h_ref[rows, :], g_ref[...]).astype(BF16)
        v = _gelu(_dot(xn_ref[rows, :], win_ref[:, c_dim:2 * c_dim]) + bin_ref[:, c_dim:2 * c_dim])
        v_ref[rows, :] = _layer_norm(v, lng_ref[...], lnb_ref[...]).astype(BF16)
    for n in range(tm // ROW_BLOCK):
        rows = slice(n * ROW_BLOCK, (n + 1) * ROW_BLOCK)
        u_ref[rows, :] = _gelu(_dot(xn_ref[rows, :], win_ref[:, 0:c_dim]) + bin_ref[:, 0:c_dim])

    row = lax.broadcasted_iota(jnp.int32, (chunk, chunk), 0)
    col = lax.broadcasted_iota(jnp.int32, (chunk, chunk), 1)
    causal = col <= row
    sv = []
    for gi in range(groups):
        lanes = slice(gi * gdim, (gi + 1) * gdim)
        ws = jnp.where(causal, ws_ref[gi], 0.0).astype(BF16)
        vg = jnp.concatenate(
            [v_ref[n * chunk:(n + 1) * chunk, lanes] for n in range(n_chunks)], axis=1)
        sv.append(_dot(ws, vg))
    chunks_per_block = ROW_BLOCK // chunk
    for blk in range(tm // ROW_BLOCK):
        y = []
        for n in range(blk * chunks_per_block, (blk + 1) * chunks_per_block):
            gate = jnp.concatenate(
                [sv[gi][:, n * gdim:(n + 1) * gdim] + bs_ref[gi] for gi in range(groups)], axis=1)
            y.append((u_ref[n * chunk:(n + 1) * chunk, :] * gate).astype(BF16))
        rows = slice(blk * ROW_BLOCK, (blk + 1) * ROW_BLOCK)
        o_ref[rows, :] = h_ref[rows, :] + _dot(jnp.concatenate(y, axis=0), wout_ref[...])


def _spatial_gating(h, g, w_in, b_in, ln_v_g, ln_v_b, w_s, b_s, w_out, *, seq_len):
    t, d = h.shape
    c_dim = w_out.shape[0]
    groups, chunk, _ = w_s.shape
    gdim = c_dim // groups
    assert TOKEN_TILE % chunk == 0 and seq_len % TOKEN_TILE == 0
    bs = jnp.broadcast_to(b_s[:, :, None], (groups, chunk, gdim))
    args = (h, g.reshape(1, d), w_in, b_in.reshape(1, 2 * c_dim),
            ln_v_g.reshape(1, c_dim), ln_v_b.reshape(1, c_dim), w_s, bs, w_out)
    in_specs = [pl.BlockSpec((TOKEN_TILE, d), lambda i: (i, 0))]
    in_specs += [_resident(a.shape) for a in args[1:]]
    return pl.pallas_call(
        functools.partial(_spatial_body, c_dim=c_dim),
        grid=(t // TOKEN_TILE,),
        in_specs=in_specs,
        out_specs=pl.BlockSpec((TOKEN_TILE, d), lambda i: (i, 0)),
        out_shape=jax.ShapeDtypeStruct((t, d), F32),
        scratch_shapes=[
            pltpu.VMEM((TOKEN_TILE, d), BF16),
            pltpu.VMEM((TOKEN_TILE, c_dim), F32),
            pltpu.VMEM((TOKEN_TILE, c_dim), BF16),
            pltpu.VMEM((TOKEN_TILE, c_dim), BF16),
        ],
        compiler_params=pltpu.CompilerParams(
            dimension_semantics=("arbitrary",), vmem_limit_bytes=VMEM_LIMIT_BYTES),
        name="spatial_gating",
    )(*args)


def kernel(x, ev_norm_g, ev_w_in, ev_conv_a_w, ev_conv_a_b, ev_ln_a_g, ev_ln_a_b, ev_conv_b_w, ev_w_out, od_norm_g, od_w_in, od_b_in, od_ln_v_g, od_ln_v_b, od_w_s, od_b_s, od_w_out, mlp_norm_g, mlp_w1, mlp_w2, final_norm_g):
    bsz, s, d = x.shape
    depth = mlp_w1.shape[0]
    a_dim = ev_conv_a_w.shape[2]
    b_dim = ev_conv_b_w.shape[2]

    def layer(stack, k, stream):
        return _Layer(stack, k) if stream else stack[k]

    def mixer_weights(i, stream):
        j = i // 2
        if i % 2 == 0:
            return (_conv_in_columns(ev_w_in[j], a_dim, b_dim), layer(ev_w_out, j, stream))
        return (layer(od_w_in, j, stream), layer(od_w_out, j, stream))

    def to_bf16(ws):
        return tuple(w.astype(BF16) for w in ws)

    mix_w = to_bf16(mixer_weights(0, stream=False))
    mlp_w = None
    h = x.reshape(bsz * s, d)
    for i in range(depth):
        j = i // 2
        if i % 2 == 0:
            h, mlp_w = _conv_mix(h, ev_norm_g[j], mix_w[0], ev_conv_a_w[j], ev_conv_a_b[j],
                                 ev_ln_a_g[j], ev_ln_a_b[j], ev_conv_b_w[j], mix_w[1],
                                 seq_len=s, cast=(_Layer(mlp_w1, i), _Layer(mlp_w2, i)))
        else:
            h = _spatial_gating(h, od_norm_g[j], mix_w[0], od_b_in[j], od_ln_v_g[j], od_ln_v_b[j],
                                od_w_s[j], od_b_s[j], mix_w[1], seq_len=s)
            if mlp_w is None:
                mlp_w = to_bf16((mlp_w1[i], mlp_w2[i]))
        last = i == depth - 1
        cast = ()
        if not last:
            cast = mixer_weights(i + 1, stream=True)
            if (i + 1) % 2 == 1:
                cast = cast + (_Layer(mlp_w1, i + 1), _Layer(mlp_w2, i + 1))
        h, staged = _mlp(h, mlp_norm_g[i], mlp_w[0], mlp_w[1], final_norm_g if last else None,
                         cast=cast)
        mix_w, mlp_w = staged[0:2], (staged[2:4] or None)
    return h.reshape(bsz, s, d)
```
